```python
import jax
import jax.numpy as jnp
from jax import lax
import numpy as np

D_MODEL = 1024
BATCH = 8
SEQ = 2048
DEPTH = 2

CTX_LEN = 256
GRID_W = 64
EPS = 1e-6
ALPHA = (2 * DEPTH) ** 0.25
BETA = (8 * DEPTH) ** -0.25
N_MOD = 6

CONV_CH = D_MODEL // 2
CONV_WIDTH = 31
ATT_HEADS = 8
ATT_KV_HEADS = 2
HEAD_DIM = 64
ATT_W = ATT_HEADS * HEAD_DIM
KV_W = ATT_KV_HEADS * HEAD_DIM
Q_BLOCK = 128
ROPE_BASE = 10000.0
AB_IN = 2 * CONV_CH + ATT_W + 2 * KV_W
AB_OUT = CONV_CH + ATT_W

ML_HEADS = 4
ML_DQK = D_MODEL // 8
ML_DV = D_MODEL // 4
ML_CHUNK = 64
ML_QK_W = ML_HEADS * ML_DQK
ML_V_W = ML_HEADS * ML_DV
C_IN = 2 * ML_QK_W + 2 * ML_V_W + 4 * ML_HEADS
C_OUT = ML_V_W

PEER_HEADS = 8
PEER_DK = 256
N_KEYS = 128
N_EXPERTS = N_KEYS * N_KEYS
PEER_TOPK = 16
PEER_BLOCK = 128

N_EVEN = (DEPTH + 1) // 2
N_ODD = DEPTH // 2

kernel_name = 'hybrid_conv_gqa_mlstm_peer_diffusion_block'


def layer_norm(x, g=None, b=None):
    xf = x.astype(jnp.float32)
    mu = jnp.mean(xf, axis=-1, keepdims=True)
    var = jnp.mean(jnp.square(xf - mu), axis=-1, keepdims=True)
    y = (xf - mu) * lax.rsqrt(var + EPS)
    if g is not None:
        y = y * g.astype(jnp.float32)
    if b is not None:
        y = y + b.astype(jnp.float32)
    return y.astype(x.dtype)


def rms_norm(x, g):
    xf = x.astype(jnp.float32)
    y = xf * lax.rsqrt(jnp.mean(jnp.square(xf), axis=-1, keepdims=True) + EPS)
    return (y * g.astype(jnp.float32)).astype(x.dtype)


def modulate(x, shift, scale):
    return layer_norm(x) * (1 + scale) + shift


def rope_1d(x, pos):
    f = x.shape[-1] // 2
    inv_freq = ROPE_BASE ** (-jnp.arange(f, dtype=jnp.float32) / f)
    ang = pos[:, None] * inv_freq[None, :]
    cos = jnp.cos(ang)[None, :, None, :]
    sin = jnp.sin(ang)[None, :, None, :]
    xf = x.astype(jnp.float32)
    x1, x2 = xf[..., :f], xf[..., f:]
    return jnp.concatenate([x1 * cos - x2 * sin, x2 * cos + x1 * sin], axis=-1).astype(x.dtype)


def axial_rope(x, row, col):
    half = x.shape[-1] // 2
    return jnp.concatenate([rope_1d(x[..., :half], row), rope_1d(x[..., half:], col)], axis=-1)


def grid_positions(n_tokens):
    rows = n_tokens // GRID_W
    row = jnp.repeat(jnp.arange(rows, dtype=jnp.float32), GRID_W)
    col = jnp.tile(jnp.arange(GRID_W, dtype=jnp.float32), rows)
    return row, col


def depthwise_conv(x, w, b):
    pad = CONV_WIDTH // 2
    y = lax.conv_general_dilated(x, w[:, None, :].astype(x.dtype), window_strides=(1,),
                                 padding=[(pad, pad)], dimension_numbers=('NWC', 'WIO', 'NWC'),
                                 feature_group_count=x.shape[-1])
    return y + b


def gqa_attend(q, k, v):
    s = jnp.einsum('bqkgd,bskd->bkgqs', q, k).astype(jnp.float32)
    p = jax.nn.softmax(s, axis=-1).astype(v.dtype)
    return jnp.einsum('bkgqs,bskd->bqkgd', p, v)


def mixer_conv_gqa(hc, hl, w_in, b_in, conv_w, conv_b, conv_ln_g, conv_ln_b, q_norm_g, k_norm_g,
                   w_out, b_out, need_ctx):
    B, L, _ = hl.shape
    Lc = hc.shape[1]
    G = ATT_HEADS // ATT_KV_HEADS
    cuts = [CONV_CH, 2 * CONV_CH, 2 * CONV_CH + ATT_W, 2 * CONV_CH + ATT_W + KV_W]

    def project(h):
        n = h.shape[1]
        a_val, a_gate, q, k, v = jnp.split(h @ w_in + b_in, cuts, axis=-1)
        q = rms_norm(q.reshape(B, n, ATT_HEADS, HEAD_DIM), q_norm_g)
        k = rms_norm(k.reshape(B, n, ATT_KV_HEADS, HEAD_DIM), k_norm_g)
        v = v.reshape(B, n, ATT_KV_HEADS, HEAD_DIM)
        return a_val, a_gate, q, k, v

    def conv_group(a_val, a_gate):
        u = depthwise_conv(a_val * jax.nn.sigmoid(a_gate), conv_w, conv_b)
        return jax.nn.silu(layer_norm(u, conv_ln_g, conv_ln_b))

    def merge(conv_out, att_out):
        return jnp.concatenate([conv_out, att_out], axis=-1) @ w_out + b_out

    ac_v, ac_g, qc, kc, vc = project(hc)
    al_v, al_g, ql, kl, vl = project(hl)
    row, col = grid_positions(L)
    ql = axial_rope(ql, row, col)
    kl = axial_rope(kl, row, col)
    scale = HEAD_DIM ** -0.5
    k_all = jnp.concatenate([kc, kl], axis=1)
    v_all = jnp.concatenate([vc, vl], axis=1)
    q_blocks = (ql * scale).reshape(B, L // Q_BLOCK, Q_BLOCK, ATT_KV_HEADS, G, HEAD_DIM).transpose(1, 0, 2, 3, 4, 5)
    o_blocks = lax.map(lambda qb: gqa_attend(qb, k_all, v_all), q_blocks)
    ol = o_blocks.transpose(1, 0, 2, 3, 4, 5).reshape(B, L, ATT_W)
    yl = merge(conv_group(al_v, al_g), ol)
    yc = None
    if need_ctx:
        oc = gqa_attend((qc * scale).reshape(B, Lc, ATT_KV_HEADS, G, HEAD_DIM), kc, vc).reshape(B, Lc, ATT_W)
        yc = merge(conv_group(ac_v, ac_g), oc)
    return yc, yl


def mlstm_scan(q, k, v, ig, lf, state):
    B, L = q.shape[:2]
    nc = L // ML_CHUNK

    def chunks(t):
        t = t.reshape((B, nc, ML_CHUNK) + t.shape[2:])
        return jnp.moveaxis(jnp.moveaxis(t, 1, 0), 3, 2)

    mask = jnp.tril(jnp.ones((ML_CHUNK, ML_CHUNK), dtype=bool))

    def step(carry, inp):
        C, n, m = carry
        qc, kc, vc, ic, fc = inp
        b = jnp.cumsum(fc, axis=-1)
        a = b + m[..., None]
        dlog = jnp.where(mask, b[..., :, None] - b[..., None, :] + ic[..., None, :], -jnp.inf)
        mt = jnp.maximum(a, jnp.max(dlog, axis=-1))
        w = jnp.exp(dlog - mt[..., None])
        aw = jnp.exp(a - mt)
        s = jnp.einsum('bhtd,bhsd->bhts', qc, kc) * w
        num = jnp.einsum('bhts,bhsv->bhtv', s, vc) + aw[..., None] * jnp.einsum('bhtd,bhvd->bhtv', qc, C)
        den = jnp.sum(s, axis=-1) + aw * jnp.einsum('bhtd,bhd->bht', qc, n)
        h = num / jnp.maximum(jnp.abs(den), jnp.exp(-mt))[..., None]
        bl = b[..., -1]
        wl = bl[..., None] - b + ic
        mn = jnp.maximum(bl + m, jnp.max(wl, axis=-1))
        wc = jnp.exp(wl - mn[..., None])
        dc = jnp.exp(bl + m - mn)
        C = dc[..., None, None] * C + jnp.einsum('bhs,bhsv,bhsd->bhvd', wc, vc, kc)
        n = dc[..., None] * n + jnp.einsum('bhs,bhsd->bhd', wc, kc)
        return (C, n, mn), h

    state, h = lax.scan(step, state, (chunks(q), chunks(k), chunks(v), chunks(ig), chunks(lf)))
    h = jnp.swapaxes(jnp.moveaxis(h, 0, 1), 2, 3).reshape(B, L, ML_HEADS, ML_DV)
    return state, h


def mixer_mlstm(hc, hl, w_in, b_in, norm_g, w_out, b_out, need_ctx):
    B = hl.shape[0]
    cuts = [ML_QK_W, 2 * ML_QK_W, 2 * ML_QK_W + ML_V_W, 2 * ML_QK_W + 2 * ML_V_W]
    f32 = jnp.float32

    def project(h):
        n = h.shape[1]
        q, k, v, o, g = jnp.split(h @ w_in + b_in, cuts, axis=-1)
        q = q.reshape(B, n, ML_HEADS, ML_DQK).astype(f32)
        k = k.reshape(B, n, ML_HEADS, ML_DQK).astype(f32) * (ML_DQK ** -0.5)
        v = v.reshape(B, n, ML_HEADS, ML_DV).astype(f32)
        g = g.reshape(B, n, 4, ML_HEADS).astype(f32)
        fwd = (g[:, :, 0], jax.nn.log_sigmoid(g[:, :, 1]))
        bwd = (g[:, :, 2], jax.nn.log_sigmoid(g[:, :, 3]))
        return q, k, v, o, fwd, bwd

    def flip(t):
        return jnp.flip(t, axis=1)

    qc, kc, vc, oc, gc_f, gc_b = project(hc)
    ql, kl, vl, ol, gl_f, gl_b = project(hl)
    init = (jnp.zeros((B, ML_HEADS, ML_DV, ML_DQK), f32), jnp.zeros((B, ML_HEADS, ML_DQK), f32),
            jnp.zeros((B, ML_HEADS), f32))
    st_cf, h_cf = mlstm_scan(qc, kc, vc, gc_f[0], gc_f[1], init)
    _, h_lf = mlstm_scan(ql, kl, vl, gl_f[0], gl_f[1], st_cf)
    st_cb, h_cb = mlstm_scan(flip(qc), flip(kc), flip(vc), flip(gc_b[0]), flip(gc_b[1]), init)
    _, h_lb = mlstm_scan(flip(ql), flip(kl), flip(vl), flip(gl_b[0]), flip(gl_b[1]), st_cb)

    def finish(hf, hb, o):
        n = o.shape[1]
        hn = layer_norm(hf + hb, norm_g.reshape(ML_HEADS, ML_DV))
        y = hn.reshape(B, n, C_OUT).astype(o.dtype) * jax.nn.sigmoid(o)
        return y @ w_out + b_out

    yl = finish(h_lf, flip(h_lb), ol)
    yc = finish(h_cf, flip(h_cb), oc) if need_ctx else None
    return yc, yl


def peer(h, w_q, sub_k1, sub_k2, u_tab, v_tab):
    B, L, D = h.shape
    xt = h.reshape(B * L // PEER_BLOCK, PEER_BLOCK, D)

    def block(xb):
        qh = (xb @ w_q).reshape(PEER_BLOCK, PEER_HEADS, 2, PEER_DK // 2)
        s1 = jnp.einsum('thd,hkd->thk', qh[:, :, 0], sub_k1).astype(jnp.float32)
        s2 = jnp.einsum('thd,hkd->thk', qh[:, :, 1], sub_k2).astype(jnp.float32)
        v1, i1 = lax.top_k(s1, PEER_TOPK)
        v2, i2 = lax.top_k(s2, PEER_TOPK)
        cand = (v1[..., :, None] + v2[..., None, :]).reshape(PEER_BLOCK, PEER_HEADS, PEER_TOPK * PEER_TOPK)
        cid = (i1[..., :, None] * N_KEYS + i2[..., None, :]).reshape(PEER_BLOCK, PEER_HEADS, PEER_TOPK * PEER_TOPK)
        sc, pos = lax.top_k(cand, PEER_TOPK)
        eid = jnp.take_along_axis(cid, pos, axis=-1)
        g = jax.nn.softmax(sc, axis=-1)
        act = jax.nn.gelu(jnp.einsum('thed,td->the', u_tab[eid], xb).astype(jnp.float32), approximate=False)
        return jnp.einsum('the,thed->td', (g * act).astype(xb.dtype), v_tab[eid])

    return lax.map(block, xt).reshape(B, L, D)


def setup_inputs(seed: int = 0) -> dict:
    key = jax.random.key(seed)
    keys = iter(list(jax.random.split(key, 40)))

    def nrm(shape, s):
        return jax.random.normal(next(keys), shape, jnp.float32) * s

    D = D_MODEL
    g0 = C_IN - 4 * ML_HEADS
    forget_bias = jnp.linspace(3.0, 6.0, ML_HEADS, dtype=jnp.float32)
    ml_b_in = nrm((N_ODD, C_IN), 0.02)
    ml_b_in = ml_b_in.at[:, g0 + ML_HEADS:g0 + 2 * ML_HEADS].add(forget_bias).at[:, g0 + 3 * ML_HEADS:].add(forget_bias)
    return {
        'x': nrm((BATCH, SEQ, D), 1.0),
        'c': nrm((BATCH, D), 1.0),
        'ctx': nrm((BATCH, CTX_LEN, D), 1.0),
        'c_ctx': nrm((D,), 1.0),
        'w_mod': nrm((DEPTH, D, N_MOD * D), D ** -0.5),
        'b_mod': nrm((DEPTH, N_MOD * D), 0.02),
        'ln1_g': 1.0 + nrm((DEPTH, D), 0.02),
        'ln1_b': nrm((DEPTH, D), 0.02),
        'ln2_g': 1.0 + nrm((DEPTH, D), 0.02),
        'ln2_b': nrm((DEPTH, D), 0.02),
        'ab_w_in': nrm((N_EVEN, D, AB_IN), D ** -0.5),
        'ab_b_in': nrm((N_EVEN, AB_IN), 0.02),
        'ab_conv_w': nrm((N_EVEN, CONV_WIDTH, CONV_CH), CONV_WIDTH ** -0.5),
        'ab_conv_b': nrm((N_EVEN, CONV_CH), 0.02),
        'ab_conv_ln_g': 1.0 + nrm((N_EVEN, CONV_CH), 0.02),
        'ab_conv_ln_b': nrm((N_EVEN, CONV_CH), 0.02),
        'ab_q_norm_g': 1.0 + nrm((N_EVEN, HEAD_DIM), 0.02),
        'ab_k_norm_g': 1.0 + nrm((N_EVEN, HEAD_DIM), 0.02),
        'ab_w_out': nrm((N_EVEN, AB_OUT, D), BETA * AB_OUT ** -0.5),
        'ab_b_out': nrm((N_EVEN, D), 0.02),
        'ml_w_in': nrm((N_ODD, D, C_IN), D ** -0.5),
        'ml_b_in': ml_b_in,
        'ml_norm_g': 1.0 + nrm((N_ODD, C_OUT), 0.02),
        'ml_w_out': nrm((N_ODD, C_OUT, D), BETA * C_OUT ** -0.5),
        'ml_b_out': nrm((N_ODD, D), 0.02),
        'peer_w_q': nrm((DEPTH, D, PEER_HEADS * PEER_DK), D ** -0.5),
        'peer_k1': nrm((DEPTH, PEER_HEADS, N_KEYS, PEER_DK // 2), (PEER_DK // 2) ** -0.5),
        'peer_k2': nrm((DEPTH, PEER_HEADS, N_KEYS, PEER_DK // 2), (PEER_DK // 2) ** -0.5),
        'peer_u': nrm((DEPTH, N_EXPERTS, D), D ** -0.5),
        'peer_v': nrm((DEPTH, N_EXPERTS, D), BETA),
    }


def reference(x, c, ctx, c_ctx, w_mod, b_mod, ln1_g, ln1_b, ln2_g, ln2_b,
              ab_w_in, ab_b_in, ab_conv_w, ab_conv_b, ab_conv_ln_g, ab_conv_ln_b, ab_q_norm_g, ab_k_norm_g,
              ab_w_out, ab_b_out, ml_w_in, ml_b_in, ml_norm_g, ml_w_out, ml_b_out,
              peer_w_q, peer_k1, peer_k2, peer_u, peer_v):
    for i in range(DEPTH):
        need_ctx = i < DEPTH - 1
        j = i // 2
        mod_l = (jax.nn.silu(c) @ w_mod[i] + b_mod[i])[:, None, :]
        mod_c = (jax.nn.silu(c_ctx) @ w_mod[i] + b_mod[i])[None, None, :]
        sh1l, sc1l, g1l, sh2l, sc2l, g2l = jnp.split(mod_l, N_MOD, axis=-1)
        sh1c, sc1c, g1c, sh2c, sc2c, g2c = jnp.split(mod_c, N_MOD, axis=-1)
        hl = modulate(x, sh1l, sc1l)
        hc = modulate(ctx, sh1c, sc1c)
        if i % 2 == 0:
            yc, yl = mixer_conv_gqa(hc, hl, ab_w_in[j], ab_b_in[j], ab_conv_w[j], ab_conv_b[j],
                                    ab_conv_ln_g[j], ab_conv_ln_b[j], ab_q_norm_g[j], ab_k_norm_g[j],
                                    ab_w_out[j], ab_b_out[j], need_ctx)
        else:
            yc, yl = mixer_mlstm(hc, hl, ml_w_in[j], ml_b_in[j], ml_norm_g[j], ml_w_out[j], ml_b_out[j], need_ctx)
        x = layer_norm(ALPHA * x + g1l * yl, ln1_g[i], ln1_b[i])
        x = layer_norm(ALPHA * x + g2l * peer(modulate(x, sh2l, sc2l), peer_w_q[i], peer_k1[i], peer_k2[i],
                                              peer_u[i], peer_v[i]), ln2_g[i], ln2_b[i])
        if need_ctx:
            ctx = layer_norm(ALPHA * ctx + g1c * yc, ln1_g[i], ln1_b[i])
            ctx = layer_norm(ALPHA * ctx + g2c * peer(modulate(ctx, sh2c, sc2c), peer_w_q[i], peer_k1[i],
                                                      peer_k2[i], peer_u[i], peer_v[i]), ln2_g[i], ln2_b[i])
    return x
```

```python
import functools
import math

import jax
import jax.numpy as jnp
from jax import lax
from jax.experimental import pallas as pl
from jax.experimental.pallas import tpu as pltpu

F32 = jnp.float32
BF16 = jnp.bfloat16

DEPTH = 2
EPS = 1e-6
ALPHA = (2 * DEPTH) ** 0.25
N_MOD = 6
GRID_W = 64

CONV_WIDTH = 31
ATT_HEADS = 8
ATT_KV_HEADS = 2
HEAD_DIM = 64
ROPE_BASE = 10000.0

ML_HEADS = 4
ML_CHUNK = 64

PEER_HEADS = 8
N_KEYS = 128
PEER_TOPK = 16

V7X_VMEM_BYTES = 64 * 1024 * 1024
V7X_LANES = 128
NEG_INF = float("-inf")


def _params(semantics, vmem_bytes):
    return pltpu.CompilerParams(dimension_semantics=semantics,
                                vmem_limit_bytes=min(int(vmem_bytes), V7X_VMEM_BYTES - 8 * 1024 * 1024))


def _ln(x):
    mu = jnp.mean(x, axis=-1, keepdims=True)
    xc = x - mu
    var = jnp.mean(xc * xc, axis=-1, keepdims=True)
    return xc * lax.rsqrt(var + EPS)


def _dot(a, b):
    return jnp.dot(a, b, preferred_element_type=F32)


def _dot_nt(a, b):
    return lax.dot_general(a, b, (((1,), (1,)), ((), ())), preferred_element_type=F32)


def _dot_tn(a, b):
    return lax.dot_general(a, b, (((0,), (0,)), ((), ())), preferred_element_type=F32)


def _mod_kernel(c_ref, w_ref, b_ref, o_ref):
    c = c_ref[...]
    s = (c * jax.nn.sigmoid(c)).astype(BF16)
    o_ref[0] = _dot(s, w_ref[0].astype(BF16)) + b_ref[0]


def _modulation(c_rows, w_mod, b_mod):
    depth, d, n = w_mod.shape
    rows = c_rows.shape[0]
    tn = 1536
    return pl.pallas_call(
        _mod_kernel,
        grid=(depth, n // tn),
        in_specs=[pl.BlockSpec((rows, d), lambda i, j: (0, 0)),
                  pl.BlockSpec((1, d, tn), lambda i, j: (i, 0, j)),
                  pl.BlockSpec((1, 1, tn), lambda i, j: (i, 0, j))],
        out_specs=pl.BlockSpec((1, rows, tn), lambda i, j: (i, 0, j)),
        out_shape=jax.ShapeDtypeStruct((depth, rows, n), F32),
        compiler_params=_params(("parallel", "parallel"), 4 * d * tn * 4),
        name="modulation",
    )(c_rows, w_mod, b_mod.reshape(depth, 1, n))


def _rope(x, cos, sin_signed):
    w = x.shape[-1]
    lane = lax.broadcasted_iota(jnp.int32, x.shape, 1)
    first = (lane % 32) < 16
    partner = jnp.where(first, pltpu.roll(x, w - 16, 1), pltpu.roll(x, 16, 1))
    return x * cos + partner * sin_signed


def _group_rms(x, bd, g):
    w = bd.shape[0]
    parts = []
    for s in range(0, x.shape[-1], w):
        xs = x[:, s:s + w]
        ms = _dot((xs * xs).astype(BF16), bd) * (1.0 / HEAD_DIM)
        parts.append(xs * lax.rsqrt(ms + EPS))
    y = parts[0] if len(parts) == 1 else jnp.concatenate(parts, axis=-1)
    return y * g


def _ab_in_kernel(use_rope, conv_ch, att_w, kv_w, x_ref, sh_ref, sc_ref, w_ref, b_ref, qg_ref, kg_ref, bd_ref,
                  cos_ref, sin_ref, u_ref, q_ref, k_ref, v_ref):
    h = _ln(x_ref[0]) * (1.0 + sc_ref[0]) + sh_ref[0]
    p = _dot(h.astype(BF16), w_ref[...]) + b_ref[...]
    c1, c2, c3, c4 = conv_ch, 2 * conv_ch, 2 * conv_ch + att_w, 2 * conv_ch + att_w + kv_w
    u_ref[0] = p[:, :c1] * jax.nn.sigmoid(p[:, c1:c2])
    q = _group_rms(p[:, c2:c3], bd_ref[...], qg_ref[...])
    k = _group_rms(p[:, c3:c4], bd_ref[:kv_w, :kv_w], kg_ref[...])
    if use_rope:
        q = _rope(q, cos_ref[...], sin_ref[...])
        k = _rope(k, cos_ref[:, :kv_w], sin_ref[:, :kv_w])
    q_ref[0] = (q * (HEAD_DIM ** -0.5)).astype(BF16)
    k_ref[0] = k.astype(BF16)
    v_ref[0] = p[:, c4:].astype(BF16)


def _ab_in(x, sh, sc, w, b, qg, kg, bd, cos, sin, use_rope, conv_ch, att_w, kv_w):
    bn, l, d = x.shape
    n = w.shape[1]
    tm = min(512, l)
    kern = functools.partial(_ab_in_kernel, use_rope, conv_ch, att_w, kv_w)
    row = lambda i, j: (i, j, 0)
    mod = lambda i, j: (i, 0, 0)
    fix = lambda i, j: (0, 0)
    return pl.pallas_call(
        kern,
        grid=(bn, l // tm),
        in_specs=[pl.BlockSpec((1, tm, d), row), pl.BlockSpec((1, 1, d), mod), pl.BlockSpec((1, 1, d), mod),
                  pl.BlockSpec((d, n), fix), pl.BlockSpec((1, n), fix),
                  pl.BlockSpec((1, att_w), fix), pl.BlockSpec((1, kv_w), fix), pl.BlockSpec(bd.shape, fix),
                  pl.BlockSpec((tm, att_w), lambda i, j: (j, 0)), pl.BlockSpec((tm, att_w), lambda i, j: (j, 0))],
        out_specs=[pl.BlockSpec((1, tm, conv_ch), row), pl.BlockSpec((1, tm, att_w), row),
                   pl.BlockSpec((1, tm, kv_w), row), pl.BlockSpec((1, tm, kv_w), row)],
        out_shape=[jax.ShapeDtypeStruct((bn, l, conv_ch), F32), jax.ShapeDtypeStruct((bn, l, att_w), BF16),
                   jax.ShapeDtypeStruct((bn, l, kv_w), BF16), jax.ShapeDtypeStruct((bn, l, kv_w), BF16)],
        compiler_params=_params(("parallel", "parallel"), 2 * (tm * d * 4 + d * n * 2) + 8 * tm * n * 4),
        name="ab_in_proj",
    )(x, sh, sc, w, b, qg, kg, bd, cos, sin)


def _attn_kernel(q_ref, k_ref, v_ref, o_ref):
    group = ATT_HEADS // ATT_KV_HEADS
    for kh in range(ATT_KV_HEADS):
        kk = k_ref[0, :, kh * HEAD_DIM:(kh + 1) * HEAD_DIM]
        vv = v_ref[0, :, kh * HEAD_DIM:(kh + 1) * HEAD_DIM]
        for g in range(group):
            lo = (kh * group + g) * HEAD_DIM
            s = _dot_nt(q_ref[0, :, lo:lo + HEAD_DIM], kk)
            p = jnp.exp(s - jnp.max(s, axis=-1, keepdims=True))
            denom = jnp.sum(p, axis=-1, keepdims=True)
            o_ref[0, :, lo:lo + HEAD_DIM] = _dot(p.astype(BF16), vv) / denom


def _attention(q, k, v):
    bn, lq, w = q.shape
    s, kvw = k.shape[1], k.shape[2]
    tq = min(256, lq)
    return pl.pallas_call(
        _attn_kernel,
        grid=(bn, lq // tq),
        in_specs=[pl.BlockSpec((1, tq, w), lambda i, j: (i, j, 0)),
                  pl.BlockSpec((1, s, kvw), lambda i, j: (i, 0, 0)),
                  pl.BlockSpec((1, s, kvw), lambda i, j: (i, 0, 0))],
        out_specs=pl.BlockSpec((1, tq, w), lambda i, j: (i, j, 0)),
        out_shape=jax.ShapeDtypeStruct((bn, lq, w), F32),
        compiler_params=_params(("parallel", "parallel"), 8 * tq * s * 4 + 8 * s * kvw * 2),
        name="gqa_attention",
    )(q, k, v)


CONV_ROWS = 32
CONV_HALO = 16


def _conv_kernel(u_ref, w_ref, cb_ref, g_ref, b_ref, o_ref, pad_ref):
    l, ch = u_ref.shape[1], u_ref.shape[2]
    zeros = jnp.zeros((CONV_HALO, ch), F32)
    pad_ref[0:CONV_HALO, :] = zeros
    pad_ref[CONV_HALO + l:2 * CONV_HALO + l, :] = zeros
    pad_ref[CONV_HALO:CONV_HALO + l, :] = u_ref[0]
    first = CONV_HALO - CONV_WIDTH // 2

    def tile(r, carry):
        base = pl.multiple_of(r * CONV_ROWS, CONV_ROWS)
        acc = jnp.zeros((CONV_ROWS, ch), F32)
        win = pad_ref[pl.ds(base, CONV_ROWS + 2 * CONV_HALO), :]
        for j in range(CONV_WIDTH):
            acc = acc + w_ref[j:j + 1, :] * win[first + j:first + j + CONV_ROWS, :]
        y = _ln(acc + cb_ref[...]) * g_ref[...] + b_ref[...]
        o_ref[0, pl.ds(base, CONV_ROWS), :] = y * jax.nn.sigmoid(y)
        return carry

    lax.fori_loop(0, l // CONV_ROWS, tile, 0)


def _conv_group(u, w, cb, g, b):
    bn, l, ch = u.shape
    fix = lambda i: (0, 0)
    return pl.pallas_call(
        _conv_kernel,
        grid=(bn,),
        in_specs=[pl.BlockSpec((1, l, ch), lambda i: (i, 0, 0)), pl.BlockSpec((CONV_WIDTH, ch), fix),
                  pl.BlockSpec((1, ch), fix), pl.BlockSpec((1, ch), fix), pl.BlockSpec((1, ch), fix)],
        out_specs=pl.BlockSpec((1, l, ch), lambda i: (i, 0, 0)),
        out_shape=jax.ShapeDtypeStruct((bn, l, ch), F32),
        scratch_shapes=[pltpu.VMEM((l + 2 * CONV_HALO, ch), F32)],
        compiler_params=_params(("parallel",), 6 * l * ch * 4),
        name="conv_group",
    )(u, w, cb, g, b)


def _resid_epilogue(y, x_ref, g1_ref, lng_ref, lnb_ref, sh2_ref, sc2_ref, x1_ref, hp_ref):
    x1 = _ln(ALPHA * x_ref[0] + g1_ref[0] * y) * lng_ref[...] + lnb_ref[...]
    x1_ref[0] = x1
    hp_ref[0] = (_ln(x1) * (1.0 + sc2_ref[0]) + sh2_ref[0]).astype(BF16)


def _ab_out_kernel(a_ref, t_ref, w1_ref, w2_ref, b_ref, x_ref, g1_ref, lng_ref, lnb_ref, sh2_ref, sc2_ref,
                   x1_ref, hp_ref):
    y = _dot(a_ref[0].astype(BF16), w1_ref[...]) + _dot(t_ref[0].astype(BF16), w2_ref[...]) + b_ref[...]
    _resid_epilogue(y, x_ref, g1_ref, lng_ref, lnb_ref, sh2_ref, sc2_ref, x1_ref, hp_ref)


def _ab_out(conv, att, w1, w2, b, x, g1, lng, lnb, sh2, sc2):
    bn, l, d = x.shape
    tm = min(512, l)
    row = lambda i, j: (i, j, 0)
    mod = lambda i, j: (i, 0, 0)
    fix = lambda i, j: (0, 0)
    k1, k2 = conv.shape[2], att.shape[2]
    return pl.pallas_call(
        _ab_out_kernel,
        grid=(bn, l // tm),
        in_specs=[pl.BlockSpec((1, tm, k1), row), pl.BlockSpec((1, tm, k2), row),
                  pl.BlockSpec((k1, d), fix), pl.BlockSpec((k2, d), fix), pl.BlockSpec((1, d), fix),
                  pl.BlockSpec((1, tm, d), row), pl.BlockSpec((1, 1, d), mod),
                  pl.BlockSpec((1, d), fix), pl.BlockSpec((1, d), fix),
                  pl.BlockSpec((1, 1, d), mod), pl.BlockSpec((1, 1, d), mod)],
        out_specs=[pl.BlockSpec((1, tm, d), row), pl.BlockSpec((1, tm, d), row)],
        out_shape=[jax.ShapeDtypeStruct((bn, l, d), F32), jax.ShapeDtypeStruct((bn, l, d), BF16)],
        compiler_params=_params(("parallel", "parallel"), 16 * tm * d * 4),
        name="ab_out_proj",
    )(conv, att, w1, w2, b, x, g1, lng, lnb, sh2, sc2)


def _ml_out_kernel(hf_ref, hb_ref, o_ref, ng_ref, w_ref, b_ref, x_ref, g1_ref, lng_ref, lnb_ref, sh2_ref, sc2_ref,
                   x1_ref, hp_ref):
    dv = hf_ref.shape[3] // ML_HEADS
    hs = hf_ref[0, 0] + hb_ref[0, 0]
    parts = [_ln(hs[:, h * dv:(h + 1) * dv]) for h in range(ML_HEADS)]
    hn = jnp.concatenate(parts, axis=-1) * ng_ref[...]
    y = _dot((hn * jax.nn.sigmoid(o_ref[0])).astype(BF16), w_ref[...]) + b_ref[...]
    _resid_epilogue(y, x_ref, g1_ref, lng_ref, lnb_ref, sh2_ref, sc2_ref, x1_ref, hp_ref)


def _ml_out(h2, o, seq_off, ng, w, b, x, g1, lng, lnb, sh2, sc2):
    bn, l, d = x.shape
    vw = o.shape[2]
    tm = math.gcd(256, math.gcd(l, seq_off))
    off = seq_off // tm
    row = lambda i, j: (i, j, 0)
    mod = lambda i, j: (i, 0, 0)
    fix = lambda i, j: (0, 0)
    return pl.pallas_call(
        _ml_out_kernel,
        grid=(bn, l // tm),
        in_specs=[pl.BlockSpec((1, 1, tm, vw), lambda i, j: (0, i, j + off, 0)),
                  pl.BlockSpec((1, 1, tm, vw), lambda i, j: (1, i, j + off, 0)),
                  pl.BlockSpec((1, tm, vw), lambda i, j: (i, j + off, 0)),
                  pl.BlockSpec((1, vw), fix), pl.BlockSpec((vw, d), fix), pl.BlockSpec((1, d), fix),
                  pl.BlockSpec((1, tm, d), row), pl.BlockSpec((1, 1, d), mod),
                  pl.BlockSpec((1, d), fix), pl.BlockSpec((1, d), fix),
                  pl.BlockSpec((1, 1, d), mod), pl.BlockSpec((1, 1, d), mod)],
        out_specs=[pl.BlockSpec((1, tm, d), row), pl.BlockSpec((1, tm, d), row)],
        out_shape=[jax.ShapeDtypeStruct((bn, l, d), F32), jax.ShapeDtypeStruct((bn, l, d), BF16)],
        compiler_params=_params(("parallel", "parallel"), 24 * tm * d * 4),
        name="ml_out_proj",
    )(h2, h2, o, ng, w, b, x, g1, lng, lnb, sh2, sc2)


def _ml_in_kernel(qk_w, v_w, x_ref, shl_ref, scl_ref, shc_ref, scc_ref, w_ref, b_ref, wg_ref, bg_ref,
                  q_ref, k_ref, v_ref, o_ref, g_ref):
    is_ctx = pl.program_id(1) == 0
    sh = jnp.where(is_ctx, shc_ref[0], shl_ref[0])
    sc = jnp.where(is_ctx, scc_ref[0], scl_ref[0])
    h = (_ln(x_ref[0]) * (1.0 + sc) + sh).astype(BF16)
    p = _dot(h, w_ref[...]) + b_ref[...]
    dqk = qk_w // ML_HEADS
    q_ref[0] = p[:, :qk_w].astype(BF16)
    k_ref[0] = (p[:, qk_w:2 * qk_w] * (dqk ** -0.5)).astype(BF16)
    v_ref[0] = p[:, 2 * qk_w:2 * qk_w + v_w].astype(BF16)
    o_ref[0] = p[:, 2 * qk_w + v_w:]
    g_ref[0] = (_dot(h, wg_ref[...]) + bg_ref[...])[:, :g_ref.shape[2]]


def _ml_in(xcat, lc, shl, scl, shc, scc, w, b, wg, bg, qk_w, v_w):
    bn, s, d = xcat.shape
    n = w.shape[1]
    ng = 4 * ML_HEADS
    tm = lc
    row = lambda i, j: (i, j, 0)
    mod = lambda i, j: (i, 0, 0)
    one = lambda i, j: (0, 0, 0)
    fix = lambda i, j: (0, 0)
    kern = functools.partial(_ml_in_kernel, qk_w, v_w)
    return pl.pallas_call(
        kern,
        grid=(bn, s // tm),
        in_specs=[pl.BlockSpec((1, tm, d), row), pl.BlockSpec((1, 1, d), mod), pl.BlockSpec((1, 1, d), mod),
                  pl.BlockSpec((1, 1, d), one), pl.BlockSpec((1, 1, d), one),
                  pl.BlockSpec((d, n), fix), pl.BlockSpec((1, n), fix),
                  pl.BlockSpec(wg.shape, fix), pl.BlockSpec(bg.shape, fix)],
        out_specs=[pl.BlockSpec((1, tm, qk_w), row), pl.BlockSpec((1, tm, qk_w), row),
                   pl.BlockSpec((1, tm, v_w), row), pl.BlockSpec((1, tm, v_w), row),
                   pl.BlockSpec((1, tm, ng), row)],
        out_shape=[jax.ShapeDtypeStruct((bn, s, qk_w), BF16), jax.ShapeDtypeStruct((bn, s, qk_w), BF16),
                   jax.ShapeDtypeStruct((bn, s, v_w), BF16), jax.ShapeDtypeStruct((bn, s, v_w), F32),
                   jax.ShapeDtypeStruct((bn, s, ng), F32)],
        compiler_params=_params(("parallel", "parallel"), 2 * (tm * d * 4 + d * n * 2) + 8 * tm * n * 4),
        name="ml_in_proj",
    )(xcat, shl, scl, shc, scc, w, b, wg, bg)


def _log_sigmoid(x):
    return jnp.minimum(x, 0.0) - jnp.log1p(jnp.exp(-jnp.abs(x)))


def _mlstm_kernel(q_ref, k_ref, v_ref, g_ref, gt_ref, h_ref, ct_ref, n_ref, m_ref):
    d = pl.program_id(0)
    step = pl.program_id(2)
    ch = q_ref.shape[1]
    dk = q_ref.shape[2] // ML_HEADS
    dv = v_ref.shape[2] // ML_HEADS

    @pl.when(step == 0)
    def _():
        ct_ref[...] = jnp.zeros_like(ct_ref)
        n_ref[...] = jnp.zeros_like(n_ref)
        m_ref[...] = jnp.zeros_like(m_ref)

    row = lax.broadcasted_iota(jnp.int32, (ch, ch), 0)
    col = lax.broadcasted_iota(jnp.int32, (ch, ch), 1)
    order = (row - col) * (1 - 2 * d)
    seen = order >= 0
    seen_t = order <= 0
    g = g_ref[0, 0]
    gt = gt_ref[0, 0, 0]

    for hd in range(ML_HEADS):
        i_col = g[:, hd:hd + 1]
        f_col = _log_sigmoid(g[:, ML_HEADS + hd:ML_HEADS + hd + 1])
        i_row = gt[hd:hd + 1, :]
        f_row = _log_sigmoid(gt[ML_HEADS + hd:ML_HEADS + hd + 1, :])
        b_col = jnp.sum(jnp.where(seen, f_row, 0.0), axis=1, keepdims=True)
        b_row = jnp.sum(jnp.where(seen_t, f_col, 0.0), axis=0, keepdims=True)
        m = m_ref[hd]
        a_col = b_col + m
        dlog = jnp.where(seen, b_col - b_row + i_row, NEG_INF)
        mt = jnp.maximum(a_col, jnp.max(dlog, axis=1, keepdims=True))
        w = jnp.exp(dlog - mt)
        aw = jnp.exp(a_col - mt)
        qh = q_ref[0, :, hd * dk:(hd + 1) * dk]
        kh = k_ref[0, :, hd * dk:(hd + 1) * dk]
        vh = v_ref[0, :, hd * dv:(hd + 1) * dv]
        smat = _dot_nt(qh, kh) * w
        ct = ct_ref[hd]
        num = _dot(smat.astype(BF16), vh) + aw * _dot(qh, ct.astype(BF16))
        qn = jnp.sum(qh.astype(F32) * n_ref[hd], axis=1, keepdims=True)
        den = jnp.sum(smat, axis=1, keepdims=True) + aw * qn
        h_ref[0, 0, :, hd * dv:(hd + 1) * dv] = num / jnp.maximum(jnp.abs(den), jnp.exp(-mt))
        bl = jnp.sum(f_row, axis=1, keepdims=True)
        wl = bl - b_col + i_col
        mn = jnp.maximum(bl + m, jnp.max(wl, axis=0, keepdims=True))
        wc = jnp.exp(wl - mn)
        dc = jnp.exp(bl + m - mn)
        kf = kh.astype(F32)
        ct_ref[hd] = dc * ct + _dot_tn(kh, (wc * vh.astype(F32)).astype(BF16))
        n_ref[hd] = dc * n_ref[hd] + jnp.sum(wc * kf, axis=0, keepdims=True)
        m_ref[hd] = mn


def _mlstm(q, k, v, g2, gt2, n_ctx_chunks):
    bn, s, qk_w = q.shape
    v_w = v.shape[2]
    ch = ML_CHUNK
    nc = s // ch
    ncx = n_ctx_chunks

    def chunk(d, t):
        back = jnp.where(t < ncx, ncx - 1 - t, nc + ncx - 1 - t)
        return jnp.where(d == 0, t, back)

    seq = lambda d, b, t: (b, chunk(d, t), 0)
    return pl.pallas_call(
        _mlstm_kernel,
        grid=(2, bn, nc),
        in_specs=[pl.BlockSpec((1, ch, qk_w), seq), pl.BlockSpec((1, ch, qk_w), seq),
                  pl.BlockSpec((1, ch, v_w), seq),
                  pl.BlockSpec((1, 1, ch, 2 * ML_HEADS), lambda d, b, t: (d, b, chunk(d, t), 0)),
                  pl.BlockSpec((1, 1, 1, 2 * ML_HEADS, ch), lambda d, b, t: (d, b, chunk(d, t), 0, 0))],
        out_specs=pl.BlockSpec((1, 1, ch, v_w), lambda d, b, t: (d, b, chunk(d, t), 0)),
        out_shape=jax.ShapeDtypeStruct((2, bn, s, v_w), F32),
        scratch_shapes=[pltpu.VMEM((ML_HEADS, qk_w // ML_HEADS, v_w // ML_HEADS), F32),
                        pltpu.VMEM((ML_HEADS, 1, qk_w // ML_HEADS), F32),
                        pltpu.VMEM((ML_HEADS, 1, 1), F32)],
        compiler_params=_params(("parallel", "parallel", "arbitrary"), 16 * 1024 * 1024),
        name="mlstm_scan",
    )(q, k, v, g2, gt2)


def _top_keys(s, vals_ref):
    rank = jnp.full(s.shape, float(PEER_TOPK), F32)
    for i in range(PEER_TOPK):
        m = jnp.max(s, axis=0, keepdims=True)
        hit = s == m
        rank = jnp.where(hit, float(i), rank)
        s = jnp.where(hit, NEG_INF, s)
        vals_ref[i:i + 1, :] = m
    return rank


def _peer_sel_kernel(hp_ref, wqt_ref, k1_ref, k2_ref, r2_ref, e2_ref, kk_ref, cc_ref, v1_ref, v2_ref, cnt_ref):
    half = N_KEYS
    qt = _dot_nt(wqt_ref[...], hp_ref[...])
    for h in range(PEER_HEADS):
        q1 = qt[(2 * h) * half:(2 * h + 1) * half].astype(BF16)
        q2 = qt[(2 * h + 1) * half:(2 * h + 2) * half].astype(BF16)
        s1 = _dot(k1_ref[h], q1)
        s2 = _dot(k2_ref[h], q2)
        r1 = _top_keys(s1, v1_ref)
        r2 = _top_keys(s2, v2_ref)
        v1 = v1_ref[...]
        v2 = v2_ref[...]
        cands = [v1 + v2[0:1]] + [v1[0:8] + v2[j:j + 1] for j in range(1, 8)] + [v1[0:1] + v2[8:16]]
        work = list(cands)
        theta = None
        for _ in range(PEER_TOPK):
            theta = functools.reduce(jnp.maximum, [jnp.max(c, axis=0, keepdims=True) for c in work])
            work = [jnp.where(c == theta, NEG_INF, c) for c in work]
        smax = v1[0:1] + v2[0:1]
        sels = [c >= theta for c in cands]
        z = functools.reduce(
            jnp.add, [jnp.sum(jnp.where(s, jnp.exp(c - smax), 0.0), axis=0, keepdims=True)
                      for s, c in zip(sels, cands)])
        cnt_ref[...] = sels[0].astype(F32)
        cnt_ref[0:8, :] += functools.reduce(jnp.add, [s.astype(F32) for s in sels[1:8]])
        cnt_ref[0:1, :] += jnp.sum(sels[8].astype(F32), axis=0, keepdims=True)
        kk = jnp.zeros(s1.shape, F32)
        for i in range(PEER_TOPK):
            kk = jnp.where(r1 == float(i), cnt_ref[i:i + 1, :], kk)
        r2_ref[0, h] = r2
        e2_ref[0, h] = jnp.exp(s2 - v2[0:1])
        kk_ref[0, h] = kk
        cc_ref[0, h] = jnp.exp(s1 - v1[0:1]) / z


def _peer_select(hp, wqt, k1, k2, t):
    nt, d = hp.shape
    nb = nt // t
    sel_spec = pl.BlockSpec((1, PEER_HEADS, N_KEYS, t), lambda i: (i, 0, 0, 0))
    sel_shape = jax.ShapeDtypeStruct((nb, PEER_HEADS, N_KEYS, t), F32)
    return pl.pallas_call(
        _peer_sel_kernel,
        grid=(nb,),
        in_specs=[pl.BlockSpec((t, d), lambda i: (i, 0)), pl.BlockSpec(wqt.shape, lambda i: (0, 0)),
                  pl.BlockSpec(k1.shape, lambda i: (0, 0, 0)), pl.BlockSpec(k2.shape, lambda i: (0, 0, 0))],
        out_specs=[sel_spec] * 4,
        out_shape=[sel_shape] * 4,
        scratch_shapes=[pltpu.VMEM((PEER_TOPK, t), F32), pltpu.VMEM((PEER_TOPK, t), F32),
                        pltpu.VMEM((PEER_TOPK, t), F32)],
        compiler_params=_params(("parallel",), 40 * 1024 * 1024),
        name="peer_select",
    )(hp, wqt, k1, k2)


def _gelu(x):
    return 0.5 * x * (1.0 + lax.erf(x * (2.0 ** -0.5)))


def _peer_dense_kernel(hp_ref, u_ref, vt_ref, r2_ref, e2_ref, kk_ref, cc_ref, x1_ref, g2_ref, lng_ref, lnb_ref,
                       x2_ref, acc_ref, w_ref):
    j = pl.program_id(1)
    tn, t = u_ref.shape[0], hp_ref.shape[0]
    na = tn // N_KEYS

    @pl.when(j == 0)
    def _():
        acc_ref[...] = jnp.zeros_like(acc_ref)

    act = _gelu(_dot_nt(u_ref[...], hp_ref[...]))
    a_rows = pl.ds(pl.multiple_of(j * na, na), na)
    for al in range(na):
        for lt in range(t // V7X_LANES):
            lanes = slice(lt * V7X_LANES, (lt + 1) * V7X_LANES)
            gsum = jnp.zeros((N_KEYS, V7X_LANES), F32)
            for h in range(PEER_HEADS):
                kk = kk_ref[0, h, a_rows, lanes][al:al + 1]
                cc = cc_ref[0, h, a_rows, lanes][al:al + 1]
                gsum = gsum + jnp.where(r2_ref[0, h, :, lanes] < kk, e2_ref[0, h, :, lanes], 0.0) * cc
            rows = slice(al * N_KEYS, (al + 1) * N_KEYS)
            w_ref[rows, lanes] = (gsum * act[rows, lanes]).astype(BF16)
    acc_ref[...] += _dot(vt_ref[...], w_ref[...])

    @pl.when(j == pl.num_programs(1) - 1)
    def _():
        y = ALPHA * x1_ref[...] + g2_ref[0] * acc_ref[...].T
        x2_ref[...] = _ln(y) * lng_ref[...] + lnb_ref[...]


def _peer_dense(hp, u, vt, sel, x1, g2, blocks_per_row, lng, lnb, t, tn):
    nt, d = hp.shape
    ne = u.shape[0]
    sel_spec = pl.BlockSpec((1, PEER_HEADS, N_KEYS, t), lambda i, j: (i, 0, 0, 0))
    fix = lambda i, j: (0, 0)
    return pl.pallas_call(
        _peer_dense_kernel,
        grid=(nt // t, ne // tn),
        in_specs=[pl.BlockSpec((t, d), lambda i, j: (i, 0)), pl.BlockSpec((tn, d), lambda i, j: (j, 0)),
                  pl.BlockSpec((d, tn), lambda i, j: (0, j)), sel_spec, sel_spec, sel_spec, sel_spec,
                  pl.BlockSpec((t, d), lambda i, j: (i, 0)),
                  pl.BlockSpec((1, 1, d), lambda i, j: (i // blocks_per_row, 0, 0)),
                  pl.BlockSpec((1, d), fix), pl.BlockSpec((1, d), fix)],
        out_specs=pl.BlockSpec((t, d), lambda i, j: (i, 0)),
        out_shape=jax.ShapeDtypeStruct((nt, d), F32),
        scratch_shapes=[pltpu.VMEM((d, t), F32), pltpu.VMEM((tn, t), BF16)],
        compiler_params=_params(("parallel", "arbitrary"), 48 * 1024 * 1024),
        name="peer_dense",
    )(hp, u, vt, *sel, x1, g2, lng, lnb)


def _peer_block(x1, hp, g2, wqt, k1, k2, u, vt, lng, lnb):
    bn, l, d = x1.shape
    t = min(512, l)
    tn = 8 * N_KEYS
    hp2 = hp.reshape(bn * l, d)
    sel = _peer_select(hp2, wqt, k1, k2, t)
    x2 = _peer_dense(hp2, u, vt, sel, x1.reshape(bn * l, d), g2, l // t, lng, lnb, t, tn)
    return x2.reshape(bn, l, d)


def _rope_tables(l, att_w):
    rows = l // GRID_W
    row = jnp.repeat(jnp.arange(rows, dtype=F32), GRID_W)
    col = jnp.tile(jnp.arange(GRID_W, dtype=F32), rows)
    f = HEAD_DIM // 4
    inv_freq = ROPE_BASE ** (-jnp.arange(f, dtype=F32) / f)
    ar = row[:, None] * inv_freq[None, :]
    ac = col[:, None] * inv_freq[None, :]
    cos = jnp.concatenate([jnp.cos(ar), jnp.cos(ar), jnp.cos(ac), jnp.cos(ac)], axis=-1)
    sin = jnp.concatenate([-jnp.sin(ar), jnp.sin(ar), -jnp.sin(ac), jnp.sin(ac)], axis=-1)
    reps = att_w // HEAD_DIM
    return jnp.tile(cos, (1, reps)), jnp.tile(sin, (1, reps))


def _block_diag_ones(width, group):
    idx = jnp.arange(width) // group
    return (idx[:, None] == idx[None, :]).astype(BF16)


def kernel(x, c, ctx, c_ctx, w_mod, b_mod, ln1_g, ln1_b, ln2_g, ln2_b, ab_w_in, ab_b_in, ab_conv_w, ab_conv_b,
           ab_conv_ln_g, ab_conv_ln_b, ab_q_norm_g, ab_k_norm_g, ab_w_out, ab_b_out, ml_w_in, ml_b_in, ml_norm_g,
           ml_w_out, ml_b_out, peer_w_q, peer_k1, peer_k2, peer_u, peer_v):
    bsz, l, d = x.shape
    lc = ctx.shape[1]
    assert l % GRID_W == 0 and lc % ML_CHUNK == 0 and l % lc == 0

    rows = -(-(bsz + 1) // 8) * 8
    c_rows = jnp.zeros((rows, d), F32).at[:bsz].set(c).at[bsz].set(c_ctx)
    mod = _modulation(c_rows, w_mod, b_mod)

    def mod_rows(i):
        m = mod[i].reshape(rows, N_MOD, d)
        lat = [m[:bsz, k][:, None, :] for k in range(N_MOD)]
        cx = [m[bsz:bsz + 1, k][:, None, :] for k in range(N_MOD)]
        return lat, cx

    row2 = lambda v: v.reshape(1, -1)

    conv_ch = ab_conv_w.shape[2]
    kv_w = ATT_KV_HEADS * HEAD_DIM
    att_w = ATT_HEADS * HEAD_DIM
    (sh1l, sc1l, g1l, sh2l, sc2l, g2l), (sh1c, sc1c, g1c, sh2c, sc2c, g2c) = mod_rows(0)
    bcast = lambda v: jnp.broadcast_to(v, (bsz, 1, d))
    w_in = ab_w_in[0].astype(BF16)
    b_in = row2(ab_b_in[0])
    qg = row2(jnp.tile(ab_q_norm_g[0], ATT_HEADS))
    kg = row2(jnp.tile(ab_k_norm_g[0], ATT_KV_HEADS))
    bd = _block_diag_ones(2 * V7X_LANES, HEAD_DIM)
    cos, sin = _rope_tables(l, att_w)
    ul, ql, kl, vl = _ab_in(x, sh1l, sc1l, w_in, b_in, qg, kg, bd, cos, sin, True, conv_ch, att_w, kv_w)
    uc, qc, kc, vc = _ab_in(ctx, bcast(sh1c), bcast(sc1c), w_in, b_in, qg, kg, bd, cos[:lc], sin[:lc], False,
                            conv_ch, att_w, kv_w)
    conv_args = (ab_conv_w[0], row2(ab_conv_b[0]), row2(ab_conv_ln_g[0]), row2(ab_conv_ln_b[0]))
    conv_l = _conv_group(ul, *conv_args)
    conv_c = _conv_group(uc, *conv_args)
    att_l = _attention(ql, jnp.concatenate([kc, kl], axis=1), jnp.concatenate([vc, vl], axis=1))
    att_c = _attention(qc, kc, vc)
    w_out = ab_w_out[0].astype(BF16)
    out_args = (w_out[:conv_ch], w_out[conv_ch:], row2(ab_b_out[0]))
    ln1 = (row2(ln1_g[0]), row2(ln1_b[0]))
    x1, hpl = _ab_out(conv_l, att_l, *out_args, x, g1l, *ln1, sh2l, sc2l)
    c1, hpc = _ab_out(conv_c, att_c, *out_args, ctx, bcast(g1c), *ln1, bcast(sh2c), bcast(sc2c))

    def peer_weights(i):
        wqt = peer_w_q[i].T.astype(BF16)
        half = peer_k1.shape[3]
        assert half == N_KEYS
        return (wqt, peer_k1[i].astype(BF16), peer_k2[i].astype(BF16), peer_u[i].astype(BF16),
                peer_v[i].T.astype(BF16), row2(ln2_g[i]), row2(ln2_b[i]))

    pw = peer_weights(0)
    x = _peer_block(x1, hpl, g2l, *pw)
    ctx = _peer_block(c1.reshape(1, bsz * lc, d), hpc.reshape(1, bsz * lc, d), g2c, *pw).reshape(bsz, lc, d)

    (sh1l, sc1l, g1l, sh2l, sc2l, g2l), (sh1c, sc1c, _, _, _, _) = mod_rows(1)
    qk_w = ML_HEADS * (d // 8)
    v_w = ML_HEADS * (d // 4)
    n_main = 2 * qk_w + 2 * v_w
    w_in = ml_w_in[0]
    wg = jnp.zeros((d, V7X_LANES), F32).at[:, :4 * ML_HEADS].set(w_in[:, n_main:]).astype(BF16)
    bg = jnp.zeros((1, V7X_LANES), F32).at[:, :4 * ML_HEADS].set(ml_b_in[0][n_main:])
    xcat = jnp.concatenate([ctx, x], axis=1)
    q, k, v, o, g = _ml_in(xcat, lc, sh1l, sc1l, sh1c, sc1c, w_in[:, :n_main].astype(BF16),
                           row2(ml_b_in[0][:n_main]), wg, bg, qk_w, v_w)
    s = lc + l
    g2 = g.reshape(bsz, s, 2, 2 * ML_HEADS).transpose(2, 0, 1, 3)
    gt2 = g2.reshape(2, bsz, s // ML_CHUNK, ML_CHUNK, 2 * ML_HEADS).transpose(0, 1, 2, 4, 3)
    h2 = _mlstm(q, k, v, g2, gt2, lc // ML_CHUNK)
    x1, hpl = _ml_out(h2, o, lc, row2(ml_norm_g[0]), ml_w_out[0].astype(BF16), row2(ml_b_out[0]), x, g1l,
                      row2(ln1_g[1]), row2(ln1_b[1]), sh2l, sc2l)
    return _peer_block(x1, hpl, g2l, *peer_weights(1))
```

```python
import functools
import math

import jax
import jax.numpy as jnp
from jax import lax
from jax.experimental import pallas as pl
from jax.experimental.pallas import tpu as pltpu

F32 = jnp.float32
BF16 = jnp.bfloat16

DEPTH = 2
EPS = 1e-6
ALPHA = (2 * DEPTH) ** 0.25
N_MOD = 6
GRID_W = 64

CONV_WIDTH = 31
ATT_HEADS = 8
ATT_KV_HEADS = 2
HEAD_DIM = 64
ROPE_BASE = 10000.0

ML_HEADS = 4
ML_CHUNK = 64

PEER_HEADS = 8
N_KEYS = 128
PEER_TOPK = 16
PEER_A_GROUP = 4

V7X_VMEM_BYTES = 64 * 1024 * 1024
V7X_LANES = 128
V7X_BF16_ROWS = 16
NEG_INF = float("-inf")


def _params(semantics, vmem_bytes):
    return pltpu.CompilerParams(dimension_semantics=semantics,
                                vmem_limit_bytes=min(int(vmem_bytes), V7X_VMEM_BYTES - 8 * 1024 * 1024))


def _ln(x):
    mu = jnp.mean(x, axis=-1, keepdims=True)
    xc = x - mu
    var = jnp.mean(xc * xc, axis=-1, keepdims=True)
    return xc * lax.rsqrt(var + EPS)


def _dot(a, b):
    return jnp.dot(a, b, preferred_element_type=F32)


def _dot_nt(a, b):
    return lax.dot_general(a, b, (((1,), (1,)), ((), ())), preferred_element_type=F32)


def _dot_tn(a, b):
    return lax.dot_general(a, b, (((0,), (0,)), ((), ())), preferred_element_type=F32)


def _mod_kernel(c_ref, w_ref, b_ref, o_ref):
    c = c_ref[...]
    s = (c * jax.nn.sigmoid(c)).astype(BF16)
    o_ref[0] = _dot(s, w_ref[0].astype(BF16)) + b_ref[0]


def _modulation(c_rows, w_mod, b_mod):
    depth, d, n = w_mod.shape
    rows = c_rows.shape[0]
    tn = 1536
    return pl.pallas_call(
        _mod_kernel,
        grid=(depth, n // tn),
        in_specs=[pl.BlockSpec((rows, d), lambda i, j: (0, 0)),
                  pl.BlockSpec((1, d, tn), lambda i, j: (i, 0, j)),
                  pl.BlockSpec((1, 1, tn), lambda i, j: (i, 0, j))],
        out_specs=pl.BlockSpec((1, rows, tn), lambda i, j: (i, 0, j)),
        out_shape=jax.ShapeDtypeStruct((depth, rows, n), F32),
        compiler_params=_params(("parallel", "parallel"), 4 * d * tn * 4),
        name="modulation",
    )(c_rows, w_mod, b_mod.reshape(depth, 1, n))


def _rope(x, cos, sin_signed):
    w = x.shape[-1]
    lane = lax.broadcasted_iota(jnp.int32, x.shape, 1)
    first = (lane % 32) < 16
    partner = jnp.where(first, pltpu.roll(x, w - 16, 1), pltpu.roll(x, 16, 1))
    return x * cos + partner * sin_signed


def _group_rms(x, bd, g):
    w = bd.shape[0]
    parts = []
    for s in range(0, x.shape[-1], w):
        xs = x[:, s:s + w]
        ms = _dot((xs * xs).astype(BF16), bd) * (1.0 / HEAD_DIM)
        parts.append(xs * lax.rsqrt(ms + EPS))
    y = parts[0] if len(parts) == 1 else jnp.concatenate(parts, axis=-1)
    return y * g


def _ab_in_kernel(use_rope, conv_ch, att_w, kv_w, x_ref, sh_ref, sc_ref, w_ref, b_ref, qg_ref, kg_ref, bd_ref,
                  cos_ref, sin_ref, u_ref, q_ref, k_ref, v_ref):
    h = _ln(x_ref[0]) * (1.0 + sc_ref[0]) + sh_ref[0]
    p = _dot(h.astype(BF16), w_ref[...]) + b_ref[...]
    c1, c2, c3, c4 = conv_ch, 2 * conv_ch, 2 * conv_ch + att_w, 2 * conv_ch + att_w + kv_w
    u_ref[0] = p[:, :c1] * jax.nn.sigmoid(p[:, c1:c2])
    q = _group_rms(p[:, c2:c3], bd_ref[...], qg_ref[...])
    k = _group_rms(p[:, c3:c4], bd_ref[:kv_w, :kv_w], kg_ref[...])
    if use_rope:
        q = _rope(q, cos_ref[...], sin_ref[...])
        k = _rope(k, cos_ref[:, :kv_w], sin_ref[:, :kv_w])
    q_ref[0] = (q * (HEAD_DIM ** -0.5)).astype(BF16)
    k_ref[0] = k.astype(BF16)
    v_ref[0] = p[:, c4:].astype(BF16)


def _ab_in(x, sh, sc, w, b, qg, kg, bd, cos, sin, use_rope, conv_ch, att_w, kv_w):
    bn, l, d = x.shape
    n = w.shape[1]
    tm = min(512, l)
    kern = functools.partial(_ab_in_kernel, use_rope, conv_ch, att_w, kv_w)
    row = lambda i, j: (i, j, 0)
    mod = lambda i, j: (i, 0, 0)
    fix = lambda i, j: (0, 0)
    return pl.pallas_call(
        kern,
        grid=(bn, l // tm),
        in_specs=[pl.BlockSpec((1, tm, d), row), pl.BlockSpec((1, 1, d), mod), pl.BlockSpec((1, 1, d), mod),
                  pl.BlockSpec((d, n), fix), pl.BlockSpec((1, n), fix),
                  pl.BlockSpec((1, att_w), fix), pl.BlockSpec((1, kv_w), fix), pl.BlockSpec(bd.shape, fix),
                  pl.BlockSpec((tm, att_w), lambda i, j: (j, 0)), pl.BlockSpec((tm, att_w), lambda i, j: (j, 0))],
        out_specs=[pl.BlockSpec((1, tm, conv_ch), row), pl.BlockSpec((1, tm, att_w), row),
                   pl.BlockSpec((1, tm, kv_w), row), pl.BlockSpec((1, tm, kv_w), row)],
        out_shape=[jax.ShapeDtypeStruct((bn, l, conv_ch), F32), jax.ShapeDtypeStruct((bn, l, att_w), BF16),
                   jax.ShapeDtypeStruct((bn, l, kv_w), BF16), jax.ShapeDtypeStruct((bn, l, kv_w), BF16)],
        compiler_params=_params(("parallel", "parallel"), 2 * (tm * d * 4 + d * n * 2) + 8 * tm * n * 4),
        name="ab_in_proj",
    )(x, sh, sc, w, b, qg, kg, bd, cos, sin)


def _attn_kernel(q_ref, k_ref, v_ref, o_ref):
    group = ATT_HEADS // ATT_KV_HEADS
    for kh in range(ATT_KV_HEADS):
        kk = k_ref[0, :, kh * HEAD_DIM:(kh + 1) * HEAD_DIM]
        vv = v_ref[0, :, kh * HEAD_DIM:(kh + 1) * HEAD_DIM]
        for g in range(group):
            lo = (kh * group + g) * HEAD_DIM
            s = _dot_nt(q_ref[0, :, lo:lo + HEAD_DIM], kk)
            p = jnp.exp(s - jnp.max(s, axis=-1, keepdims=True))
            denom = jnp.sum(p, axis=-1, keepdims=True)
            o_ref[0, :, lo:lo + HEAD_DIM] = _dot(p.astype(BF16), vv) / denom


def _attention(q, k, v):
    bn, lq, w = q.shape
    s, kvw = k.shape[1], k.shape[2]
    tq = min(256, lq)
    return pl.pallas_call(
        _attn_kernel,
        grid=(bn, lq // tq),
        in_specs=[pl.BlockSpec((1, tq, w), lambda i, j: (i, j, 0)),
                  pl.BlockSpec((1, s, kvw), lambda i, j: (i, 0, 0)),
                  pl.BlockSpec((1, s, kvw), lambda i, j: (i, 0, 0))],
        out_specs=pl.BlockSpec((1, tq, w), lambda i, j: (i, j, 0)),
        out_shape=jax.ShapeDtypeStruct((bn, lq, w), F32),
        compiler_params=_params(("parallel", "parallel"), 8 * tq * s * 4 + 8 * s * kvw * 2),
        name="gqa_attention",
    )(q, k, v)


CONV_ROWS = 32
CONV_HALO = 16


def _conv_kernel(u_ref, w_ref, cb_ref, g_ref, b_ref, o_ref, pad_ref):
    l, ch = u_ref.shape[1], u_ref.shape[2]
    zeros = jnp.zeros((CONV_HALO, ch), F32)
    pad_ref[0:CONV_HALO, :] = zeros
    pad_ref[CONV_HALO + l:2 * CONV_HALO + l, :] = zeros
    pad_ref[CONV_HALO:CONV_HALO + l, :] = u_ref[0]
    first = CONV_HALO - CONV_WIDTH // 2

    def tile(r, carry):
        base = pl.multiple_of(r * CONV_ROWS, CONV_ROWS)
        acc = jnp.zeros((CONV_ROWS, ch), F32)
        win = pad_ref[pl.ds(base, CONV_ROWS + 2 * CONV_HALO), :]
        for j in range(CONV_WIDTH):
            acc = acc + w_ref[j:j + 1, :] * win[first + j:first + j + CONV_ROWS, :]
        y = _ln(acc + cb_ref[...]) * g_ref[...] + b_ref[...]
        o_ref[0, pl.ds(base, CONV_ROWS), :] = y * jax.nn.sigmoid(y)
        return carry

    lax.fori_loop(0, l // CONV_ROWS, tile, 0)


def _conv_group(u, w, cb, g, b):
    bn, l, ch = u.shape
    fix = lambda i: (0, 0)
    return pl.pallas_call(
        _conv_kernel,
        grid=(bn,),
        in_specs=[pl.BlockSpec((1, l, ch), lambda i: (i, 0, 0)), pl.BlockSpec((CONV_WIDTH, ch), fix),
                  pl.BlockSpec((1, ch), fix), pl.BlockSpec((1, ch), fix), pl.BlockSpec((1, ch), fix)],
        out_specs=pl.BlockSpec((1, l, ch), lambda i: (i, 0, 0)),
        out_shape=jax.ShapeDtypeStruct((bn, l, ch), F32),
        scratch_shapes=[pltpu.VMEM((l + 2 * CONV_HALO, ch), F32)],
        compiler_params=_params(("parallel",), 6 * l * ch * 4),
        name="conv_group",
    )(u, w, cb, g, b)


def _resid_epilogue(y, x_ref, g1_ref, lng_ref, lnb_ref, sh2_ref, sc2_ref, x1_ref, hp_ref):
    x1 = _ln(ALPHA * x_ref[0] + g1_ref[0] * y) * lng_ref[...] + lnb_ref[...]
    x1_ref[0] = x1
    hp_ref[0] = (_ln(x1) * (1.0 + sc2_ref[0]) + sh2_ref[0]).astype(BF16)


def _ab_out_kernel(a_ref, t_ref, w1_ref, w2_ref, b_ref, x_ref, g1_ref, lng_ref, lnb_ref, sh2_ref, sc2_ref,
                   x1_ref, hp_ref):
    y = _dot(a_ref[0].astype(BF16), w1_ref[...]) + _dot(t_ref[0].astype(BF16), w2_ref[...]) + b_ref[...]
    _resid_epilogue(y, x_ref, g1_ref, lng_ref, lnb_ref, sh2_ref, sc2_ref, x1_ref, hp_ref)


def _ab_out(conv, att, w1, w2, b, x, g1, lng, lnb, sh2, sc2):
    bn, l, d = x.shape
    tm = min(512, l)
    row = lambda i, j: (i, j, 0)
    mod = lambda i, j: (i, 0, 0)
    fix = lambda i, j: (0, 0)
    k1, k2 = conv.shape[2], att.shape[2]
    return pl.pallas_call(
        _ab_out_kernel,
        grid=(bn, l // tm),
        in_specs=[pl.BlockSpec((1, tm, k1), row), pl.BlockSpec((1, tm, k2), row),
                  pl.BlockSpec((k1, d), fix), pl.BlockSpec((k2, d), fix), pl.BlockSpec((1, d), fix),
                  pl.BlockSpec((1, tm, d), row), pl.BlockSpec((1, 1, d), mod),
                  pl.BlockSpec((1, d), fix), pl.BlockSpec((1, d), fix),
                  pl.BlockSpec((1, 1, d), mod), pl.BlockSpec((1, 1, d), mod)],
        out_specs=[pl.BlockSpec((1, tm, d), row), pl.BlockSpec((1, tm, d), row)],
        out_shape=[jax.ShapeDtypeStruct((bn, l, d), F32), jax.ShapeDtypeStruct((bn, l, d), BF16)],
        compiler_params=_params(("parallel", "parallel"), 16 * tm * d * 4),
        name="ab_out_proj",
    )(conv, att, w1, w2, b, x, g1, lng, lnb, sh2, sc2)


def _ml_out_kernel(hf_ref, hb_ref, o_ref, ng_ref, w_ref, b_ref, x_ref, g1_ref, lng_ref, lnb_ref, sh2_ref, sc2_ref,
                   x1_ref, hp_ref):
    dv = hf_ref.shape[3] // ML_HEADS
    hs = hf_ref[0, 0] + hb_ref[0, 0]
    parts = [_ln(hs[:, h * dv:(h + 1) * dv]) for h in range(ML_HEADS)]
    hn = jnp.concatenate(parts, axis=-1) * ng_ref[...]
    y = _dot((hn * jax.nn.sigmoid(o_ref[0])).astype(BF16), w_ref[...]) + b_ref[...]
    _resid_epilogue(y, x_ref, g1_ref, lng_ref, lnb_ref, sh2_ref, sc2_ref, x1_ref, hp_ref)


def _ml_out(h2, o, seq_off, ng, w, b, x, g1, lng, lnb, sh2, sc2):
    bn, l, d = x.shape
    vw = o.shape[2]
    tm = math.gcd(256, math.gcd(l, seq_off))
    off = seq_off // tm
    row = lambda i, j: (i, j, 0)
    mod = lambda i, j: (i, 0, 0)
    fix = lambda i, j: (0, 0)
    return pl.pallas_call(
        _ml_out_kernel,
        grid=(bn, l // tm),
        in_specs=[pl.BlockSpec((1, 1, tm, vw), lambda i, j: (0, i, j + off, 0)),
                  pl.BlockSpec((1, 1, tm, vw), lambda i, j: (1, i, j + off, 0)),
                  pl.BlockSpec((1, tm, vw), lambda i, j: (i, j + off, 0)),
                  pl.BlockSpec((1, vw), fix), pl.BlockSpec((vw, d), fix), pl.BlockSpec((1, d), fix),
                  pl.BlockSpec((1, tm, d), row), pl.BlockSpec((1, 1, d), mod),
                  pl.BlockSpec((1, d), fix), pl.BlockSpec((1, d), fix),
                  pl.BlockSpec((1, 1, d), mod), pl.BlockSpec((1, 1, d), mod)],
        out_specs=[pl.BlockSpec((1, tm, d), row), pl.BlockSpec((1, tm, d), row)],
        out_shape=[jax.ShapeDtypeStruct((bn, l, d), F32), jax.ShapeDtypeStruct((bn, l, d), BF16)],
        compiler_params=_params(("parallel", "parallel"), 24 * tm * d * 4),
        name="ml_out_proj",
    )(h2, h2, o, ng, w, b, x, g1, lng, lnb, sh2, sc2)


def _ml_in_kernel(qk_w, v_w, x_ref, shl_ref, scl_ref, shc_ref, scc_ref, w_ref, b_ref, wg_ref, bg_ref,
                  q_ref, k_ref, v_ref, o_ref, g_ref):
    is_ctx = pl.program_id(1) == 0
    sh = jnp.where(is_ctx, shc_ref[0], shl_ref[0])
    sc = jnp.where(is_ctx, scc_ref[0], scl_ref[0])
    h = (_ln(x_ref[0]) * (1.0 + sc) + sh).astype(BF16)
    p = _dot(h, w_ref[...]) + b_ref[...]
    dqk = qk_w // ML_HEADS
    q_ref[0] = p[:, :qk_w].astype(BF16)
    k_ref[0] = (p[:, qk_w:2 * qk_w] * (dqk ** -0.5)).astype(BF16)
    v_ref[0] = p[:, 2 * qk_w:2 * qk_w + v_w].astype(BF16)
    o_ref[0] = p[:, 2 * qk_w + v_w:]
    g_ref[0] = (_dot(h, wg_ref[...]) + bg_ref[...])[:, :g_ref.shape[2]]


def _ml_in(xcat, lc, shl, scl, shc, scc, w, b, wg, bg, qk_w, v_w):
    bn, s, d = xcat.shape
    n = w.shape[1]
    ng = 4 * ML_HEADS
    tm = lc
    row = lambda i, j: (i, j, 0)
    mod = lambda i, j: (i, 0, 0)
    one = lambda i, j: (0, 0, 0)
    fix = lambda i, j: (0, 0)
    kern = functools.partial(_ml_in_kernel, qk_w, v_w)
    return pl.pallas_call(
        kern,
        grid=(bn, s // tm),
        in_specs=[pl.BlockSpec((1, tm, d), row), pl.BlockSpec((1, 1, d), mod), pl.BlockSpec((1, 1, d), mod),
                  pl.BlockSpec((1, 1, d), one), pl.BlockSpec((1, 1, d), one),
                  pl.BlockSpec((d, n), fix), pl.BlockSpec((1, n), fix),
                  pl.BlockSpec(wg.shape, fix), pl.BlockSpec(bg.shape, fix)],
        out_specs=[pl.BlockSpec((1, tm, qk_w), row), pl.BlockSpec((1, tm, qk_w), row),
                   pl.BlockSpec((1, tm, v_w), row), pl.BlockSpec((1, tm, v_w), row),
                   pl.BlockSpec((1, tm, ng), row)],
        out_shape=[jax.ShapeDtypeStruct((bn, s, qk_w), BF16), jax.ShapeDtypeStruct((bn, s, qk_w), BF16),
                   jax.ShapeDtypeStruct((bn, s, v_w), BF16), jax.ShapeDtypeStruct((bn, s, v_w), F32),
                   jax.ShapeDtypeStruct((bn, s, ng), F32)],
        compiler_params=_params(("parallel", "parallel"), 2 * (tm * d * 4 + d * n * 2) + 8 * tm * n * 4),
        name="ml_in_proj",
    )(xcat, shl, scl, shc, scc, w, b, wg, bg)


def _log_sigmoid(x):
    return jnp.minimum(x, 0.0) - jnp.log1p(jnp.exp(-jnp.abs(x)))


def _mlstm_kernel(q_ref, k_ref, v_ref, g_ref, gt_ref, h_ref, ct_ref, n_ref, m_ref):
    d = pl.program_id(0)
    step = pl.program_id(2)
    ch = q_ref.shape[1]
    dk = q_ref.shape[2] // ML_HEADS
    dv = v_ref.shape[2] // ML_HEADS

    @pl.when(step == 0)
    def _():
        ct_ref[...] = jnp.zeros_like(ct_ref)
        n_ref[...] = jnp.zeros_like(n_ref)
        m_ref[...] = jnp.zeros_like(m_ref)

    row = lax.broadcasted_iota(jnp.int32, (ch, ch), 0)
    col = lax.broadcasted_iota(jnp.int32, (ch, ch), 1)
    order = (row - col) * (1 - 2 * d)
    seen = order >= 0
    seen_t = order <= 0
    g = g_ref[0, 0]
    gt = gt_ref[0, 0, 0]

    for hd in range(ML_HEADS):
        i_col = g[:, hd:hd + 1]
        f_col = _log_sigmoid(g[:, ML_HEADS + hd:ML_HEADS + hd + 1])
        i_row = gt[hd:hd + 1, :]
        f_row = _log_sigmoid(gt[ML_HEADS + hd:ML_HEADS + hd + 1, :])
        b_col = jnp.sum(jnp.where(seen, f_row, 0.0), axis=1, keepdims=True)
        b_row = jnp.sum(jnp.where(seen_t, f_col, 0.0), axis=0, keepdims=True)
        m = m_ref[hd]
        a_col = b_col + m
        dlog = jnp.where(seen, b_col - b_row + i_row, NEG_INF)
        mt = jnp.maximum(a_col, jnp.max(dlog, axis=1, keepdims=True))
        w = jnp.exp(dlog - mt)
        aw = jnp.exp(a_col - mt)
        qh = q_ref[0, :, hd * dk:(hd + 1) * dk]
        kh = k_ref[0, :, hd * dk:(hd + 1) * dk]
        vh = v_ref[0, :, hd * dv:(hd + 1) * dv]
        smat = _dot_nt(qh, kh) * w
        ct = ct_ref[hd]
        num = _dot(smat.astype(BF16), vh) + aw * _dot(qh, ct.astype(BF16))
        qn = jnp.sum(qh.astype(F32) * n_ref[hd], axis=1, keepdims=True)
        den = jnp.sum(smat, axis=1, keepdims=True) + aw * qn
        h_ref[0, 0, :, hd * dv:(hd + 1) * dv] = num / jnp.maximum(jnp.abs(den), jnp.exp(-mt))
        bl = jnp.sum(f_row, axis=1, keepdims=True)
        wl = bl - b_col + i_col
        mn = jnp.maximum(bl + m, jnp.max(wl, axis=0, keepdims=True))
        wc = jnp.exp(wl - mn)
        dc = jnp.exp(bl + m - mn)
        kf = kh.astype(F32)
        ct_ref[hd] = dc * ct + _dot_tn(kh, (wc * vh.astype(F32)).astype(BF16))
        n_ref[hd] = dc * n_ref[hd] + jnp.sum(wc * kf, axis=0, keepdims=True)
        m_ref[hd] = mn


def _mlstm(q, k, v, g2, gt2, n_ctx_chunks):
    bn, s, qk_w = q.shape
    v_w = v.shape[2]
    ch = ML_CHUNK
    nc = s // ch
    ncx = n_ctx_chunks

    def chunk(d, t):
        back = jnp.where(t < ncx, ncx - 1 - t, nc + ncx - 1 - t)
        return jnp.where(d == 0, t, back)

    seq = lambda d, b, t: (b, chunk(d, t), 0)
    return pl.pallas_call(
        _mlstm_kernel,
        grid=(2, bn, nc),
        in_specs=[pl.BlockSpec((1, ch, qk_w), seq), pl.BlockSpec((1, ch, qk_w), seq),
                  pl.BlockSpec((1, ch, v_w), seq),
                  pl.BlockSpec((1, 1, ch, 2 * ML_HEADS), lambda d, b, t: (d, b, chunk(d, t), 0)),
                  pl.BlockSpec((1, 1, 1, 2 * ML_HEADS, ch), lambda d, b, t: (d, b, chunk(d, t), 0, 0))],
        out_specs=pl.BlockSpec((1, 1, ch, v_w), lambda d, b, t: (d, b, chunk(d, t), 0)),
        out_shape=jax.ShapeDtypeStruct((2, bn, s, v_w), F32),
        scratch_shapes=[pltpu.VMEM((ML_HEADS, qk_w // ML_HEADS, v_w // ML_HEADS), F32),
                        pltpu.VMEM((ML_HEADS, 1, qk_w // ML_HEADS), F32),
                        pltpu.VMEM((ML_HEADS, 1, 1), F32)],
        compiler_params=_params(("parallel", "parallel", "arbitrary"), 16 * 1024 * 1024),
        name="mlstm_scan",
    )(q, k, v, g2, gt2)


def _top_keys(s, vals_ref):
    rank = jnp.full(s.shape, float(PEER_TOPK), F32)
    for i in range(PEER_TOPK):
        m = jnp.max(s, axis=0, keepdims=True)
        hit = s == m
        rank = jnp.where(hit, float(i), rank)
        s = jnp.where(hit, NEG_INF, s)
        vals_ref[i:i + 1, :] = m
    return rank


def _peer_sel_kernel(hp_ref, wqt_ref, k1_ref, k2_ref, r2_ref, e2_ref, kk_ref, cc_ref, v1_ref, v2_ref, cnt_ref):
    half = N_KEYS
    qt = _dot_nt(wqt_ref[...], hp_ref[...])
    for h in range(PEER_HEADS):
        q1 = qt[(2 * h) * half:(2 * h + 1) * half].astype(BF16)
        q2 = qt[(2 * h + 1) * half:(2 * h + 2) * half].astype(BF16)
        s1 = _dot(k1_ref[h], q1)
        s2 = _dot(k2_ref[h], q2)
        r1 = _top_keys(s1, v1_ref)
        r2 = _top_keys(s2, v2_ref)
        v1 = v1_ref[...]
        v2 = v2_ref[...]
        cands = [v1 + v2[0:1]] + [v1[0:8] + v2[j:j + 1] for j in range(1, 8)] + [v1[0:1] + v2[8:16]]
        work = list(cands)
        theta = None
        for _ in range(PEER_TOPK):
            theta = functools.reduce(jnp.maximum, [jnp.max(c, axis=0, keepdims=True) for c in work])
            work = [jnp.where(c == theta, NEG_INF, c) for c in work]
        smax = v1[0:1] + v2[0:1]
        sels = [c >= theta for c in cands]
        z = functools.reduce(
            jnp.add, [jnp.sum(jnp.where(s, jnp.exp(c - smax), 0.0), axis=0, keepdims=True)
                      for s, c in zip(sels, cands)])
        cnt_ref[...] = sels[0].astype(F32)
        cnt_ref[0:8, :] += functools.reduce(jnp.add, [s.astype(F32) for s in sels[1:8]])
        cnt_ref[0:1, :] += jnp.sum(sels[8].astype(F32), axis=0, keepdims=True)
        kk = jnp.zeros(s1.shape, F32)
        for i in range(PEER_TOPK):
            kk = jnp.where(r1 == float(i), cnt_ref[i:i + 1, :], kk)
        r2 = r2.astype(BF16)
        e2 = jnp.exp(s2 - v2[0:1]).astype(BF16)
        for grp in range(N_KEYS // V7X_BF16_ROWS):
            r2_ref[0, h, grp] = r2[grp * V7X_BF16_ROWS:(grp + 1) * V7X_BF16_ROWS]
            e2_ref[0, h, grp] = e2[grp * V7X_BF16_ROWS:(grp + 1) * V7X_BF16_ROWS]
        kk_ref[0, h] = kk.astype(BF16)
        cc_ref[0, h] = (jnp.exp(s1 - v1[0:1]) / z).astype(BF16)


def _peer_select(hp, wqt, k1, k2, t):
    nt, d = hp.shape
    nb = nt // t
    sel_spec = pl.BlockSpec((1, PEER_HEADS, N_KEYS, t), lambda i: (i, 0, 0, 0))
    sel_shape = jax.ShapeDtypeStruct((nb, PEER_HEADS, N_KEYS, t), BF16)
    groups = N_KEYS // V7X_BF16_ROWS
    pk_spec = pl.BlockSpec((1, PEER_HEADS, groups, V7X_BF16_ROWS, t), lambda i: (i, 0, 0, 0, 0))
    pk_shape = jax.ShapeDtypeStruct((nb, PEER_HEADS, groups, V7X_BF16_ROWS, t), BF16)
    return pl.pallas_call(
        _peer_sel_kernel,
        grid=(nb,),
        in_specs=[pl.BlockSpec((t, d), lambda i: (i, 0)), pl.BlockSpec(wqt.shape, lambda i: (0, 0)),
                  pl.BlockSpec(k1.shape, lambda i: (0, 0, 0)), pl.BlockSpec(k2.shape, lambda i: (0, 0, 0))],
        out_specs=[pk_spec, pk_spec, sel_spec, sel_spec],
        out_shape=[pk_shape, pk_shape, sel_shape, sel_shape],
        scratch_shapes=[pltpu.VMEM((PEER_TOPK, t), F32), pltpu.VMEM((PEER_TOPK, t), F32),
                        pltpu.VMEM((PEER_TOPK, t), F32)],
        compiler_params=_params(("parallel",), 40 * 1024 * 1024),
        name="peer_select",
    )(hp, wqt, k1, k2)


def _gelu(x):
    return 0.5 * x * (1.0 + lax.erf(x * (2.0 ** -0.5)))


def _peer_dense_kernel(hp_ref, u_ref, vt_ref, r2_ref, e2_ref, kk_ref, cc_ref, x1_ref, g2_ref, lng_ref, lnb_ref,
                       x2_ref, acc_ref, w_ref, r2s_ref, e2s_ref, act_ref):
    j = pl.program_id(1)
    tn, t = u_ref.shape[0], hp_ref.shape[0]
    na = tn // N_KEYS

    @pl.when(j == 0)
    def _():
        acc_ref[...] = jnp.zeros_like(acc_ref)
        r2s_ref[...] = r2_ref[0]
        e2s_ref[...] = e2_ref[0]

    a_rows = pl.ds(pl.multiple_of(j * na, na), na)
    tile = (N_KEYS // V7X_BF16_ROWS, V7X_BF16_ROWS, V7X_LANES)
    for ag in range(0, na, PEER_A_GROUP):
        rows = slice(ag * N_KEYS, (ag + PEER_A_GROUP) * N_KEYS)
        act_ref[...] = _gelu(_dot_nt(u_ref[rows, :], hp_ref[...])).astype(BF16).reshape(act_ref.shape)

        for lt in range(t // V7X_LANES):
            lanes = slice(lt * V7X_LANES, (lt + 1) * V7X_LANES)
            gsum = [jnp.zeros(tile, BF16) for _ in range(PEER_A_GROUP)]
            for h in range(PEER_HEADS):
                r2 = r2s_ref[h, :, :, lanes]
                e2 = e2s_ref[h, :, :, lanes]
                kk = kk_ref[0, h, a_rows, lanes].astype(F32)
                cc = cc_ref[0, h, a_rows, lanes].astype(F32)
                for i in range(PEER_A_GROUP):
                    al = ag + i
                    kk_t = jnp.broadcast_to(kk[al:al + 1], tile[1:]).astype(BF16)[None]
                    cc_t = jnp.broadcast_to(cc[al:al + 1], tile[1:]).astype(BF16)[None]
                    gsum[i] = jnp.where(r2 < kk_t, gsum[i] + e2 * cc_t, gsum[i])
            for i in range(PEER_A_GROUP):
                w_ref[(ag + i) * tile[0]:(ag + i + 1) * tile[0], :, lanes] = (
                    gsum[i] * act_ref[i * tile[0]:(i + 1) * tile[0], :, lanes])
    acc_ref[...] += _dot(vt_ref[...], w_ref[...].reshape(tn, t))

    @pl.when(j == pl.num_programs(1) - 1)
    def _():
        y = ALPHA * x1_ref[...] + g2_ref[0] * acc_ref[...].T
        x2_ref[...] = _ln(y) * lng_ref[...] + lnb_ref[...]


def _peer_dense(hp, u, vt, sel, x1, g2, blocks_per_row, lng, lnb, t, tn):
    nt, d = hp.shape
    ne = u.shape[0]
    sel_spec = pl.BlockSpec((1, PEER_HEADS, N_KEYS, t), lambda i, j: (i, 0, 0, 0))
    pk_spec = pl.BlockSpec((1, PEER_HEADS, N_KEYS // V7X_BF16_ROWS, V7X_BF16_ROWS, t), lambda i, j: (i, 0, 0, 0, 0))
    fix = lambda i, j: (0, 0)
    return pl.pallas_call(
        _peer_dense_kernel,
        grid=(nt // t, ne // tn),
        in_specs=[pl.BlockSpec((t, d), lambda i, j: (i, 0)), pl.BlockSpec((tn, d), lambda i, j: (j, 0)),
                  pl.BlockSpec((d, tn), lambda i, j: (0, j)), pk_spec, pk_spec, sel_spec, sel_spec,
                  pl.BlockSpec((t, d), lambda i, j: (i, 0)),
                  pl.BlockSpec((1, 1, d), lambda i, j: (i // blocks_per_row, 0, 0)),
                  pl.BlockSpec((1, d), fix), pl.BlockSpec((1, d), fix)],
        out_specs=pl.BlockSpec((t, d), lambda i, j: (i, 0)),
        out_shape=jax.ShapeDtypeStruct((nt, d), F32),
        scratch_shapes=[pltpu.VMEM((d, t), F32), pltpu.VMEM((tn // V7X_BF16_ROWS, V7X_BF16_ROWS, t), BF16),
                        pltpu.VMEM(pk_spec.block_shape[1:], BF16), pltpu.VMEM(pk_spec.block_shape[1:], BF16),
                        pltpu.VMEM((PEER_A_GROUP * N_KEYS // V7X_BF16_ROWS, V7X_BF16_ROWS, t), BF16)],
        compiler_params=_params(("parallel", "arbitrary"), 48 * 1024 * 1024),
        name="peer_dense",
    )(hp, u, vt, *sel, x1, g2, lng, lnb)


def _peer_block(x1, hp, g2, wqt, k1, k2, u, vt, lng, lnb):
    bn, l, d = x1.shape
    t = min(512, l)
    tn = 16 * N_KEYS
    hp2 = hp.reshape(bn * l, d)
    sel = _peer_select(hp2, wqt, k1, k2, t)
    x2 = _peer_dense(hp2, u, vt, sel, x1.reshape(bn * l, d), g2, l // t, lng, lnb, t, tn)
    return x2.reshape(bn, l, d)


def _rope_tables(l, att_w):
    rows = l // GRID_W
    row = jnp.repeat(jnp.arange(rows, dtype=F32), GRID_W)
    col = jnp.tile(jnp.arange(GRID_W, dtype=F32), rows)
    f = HEAD_DIM // 4
    inv_freq = ROPE_BASE ** (-jnp.arange(f, dtype=F32) / f)
    ar = row[:, None] * inv_freq[None, :]
    ac = col[:, None] * inv_freq[None, :]
    cos = jnp.concatenate([jnp.cos(ar), jnp.cos(ar), jnp.cos(ac), jnp.cos(ac)], axis=-1)
    sin = jnp.concatenate([-jnp.sin(ar), jnp.sin(ar), -jnp.sin(ac), jnp.sin(ac)], axis=-1)
    reps = att_w // HEAD_DIM
    return jnp.tile(cos, (1, reps)), jnp.tile(sin, (1, reps))


def _block_diag_ones(width, group):
    idx = jnp.arange(width) // group
    return (idx[:, None] == idx[None, :]).astype(BF16)


def kernel(x, c, ctx, c_ctx, w_mod, b_mod, ln1_g, ln1_b, ln2_g, ln2_b, ab_w_in, ab_b_in, ab_conv_w, ab_conv_b,
           ab_conv_ln_g, ab_conv_ln_b, ab_q_norm_g, ab_k_norm_g, ab_w_out, ab_b_out, ml_w_in, ml_b_in, ml_norm_g,
           ml_w_out, ml_b_out, peer_w_q, peer_k1, peer_k2, peer_u, peer_v):
    bsz, l, d = x.shape
    lc = ctx.shape[1]
    assert l % GRID_W == 0 and lc % ML_CHUNK == 0 and l % lc == 0

    rows = -(-(bsz + 1) // 8) * 8
    c_rows = jnp.zeros((rows, d), F32).at[:bsz].set(c).at[bsz].set(c_ctx)
    mod = _modulation(c_rows, w_mod, b_mod)

    def mod_rows(i):
        m = mod[i].reshape(rows, N_MOD, d)
        lat = [m[:bsz, k][:, None, :] for k in range(N_MOD)]
        cx = [m[bsz:bsz + 1, k][:, None, :] for k in range(N_MOD)]
        return lat, cx

    row2 = lambda v: v.reshape(1, -1)

    conv_ch = ab_conv_w.shape[2]
    kv_w = ATT_KV_HEADS * HEAD_DIM
    att_w = ATT_HEADS * HEAD_DIM
    (sh1l, sc1l, g1l, sh2l, sc2l, g2l), (sh1c, sc1c, g1c, sh2c, sc2c, g2c) = mod_rows(0)
    bcast = lambda v: jnp.broadcast_to(v, (bsz, 1, d))
    w_in = ab_w_in[0].astype(BF16)
    b_in = row2(ab_b_in[0])
    qg = row2(jnp.tile(ab_q_norm_g[0], ATT_HEADS))
    kg = row2(jnp.tile(ab_k_norm_g[0], ATT_KV_HEADS))
    bd = _block_diag_ones(2 * V7X_LANES, HEAD_DIM)
    cos, sin = _rope_tables(l, att_w)
    ul, ql, kl, vl = _ab_in(x, sh1l, sc1l, w_in, b_in, qg, kg, bd, cos, sin, True, conv_ch, att_w, kv_w)
    uc, qc, kc, vc = _ab_in(ctx, bcast(sh1c), bcast(sc1c), w_in, b_in, qg, kg, bd, cos[:lc], sin[:lc], False,
                            conv_ch, att_w, kv_w)
    conv_args = (ab_conv_w[0], row2(ab_conv_b[0]), row2(ab_conv_ln_g[0]), row2(ab_conv_ln_b[0]))
    conv_l = _conv_group(ul, *conv_args)
    conv_c = _conv_group(uc, *conv_args)
    att_l = _attention(ql, jnp.concatenate([kc, kl], axis=1), jnp.concatenate([vc, vl], axis=1))
    att_c = _attention(qc, kc, vc)
    w_out = ab_w_out[0].astype(BF16)
    out_args = (w_out[:conv_ch], w_out[conv_ch:], row2(ab_b_out[0]))
    ln1 = (row2(ln1_g[0]), row2(ln1_b[0]))
    x1, hpl = _ab_out(conv_l, att_l, *out_args, x, g1l, *ln1, sh2l, sc2l)
    c1, hpc = _ab_out(conv_c, att_c, *out_args, ctx, bcast(g1c), *ln1, bcast(sh2c), bcast(sc2c))

    def peer_weights(i):
        wqt = peer_w_q[i].T.astype(BF16)
        half = peer_k1.shape[3]
        assert half == N_KEYS
        return (wqt, peer_k1[i].astype(BF16), peer_k2[i].astype(BF16), peer_u[i].astype(BF16),
                peer_v[i].T.astype(BF16), row2(ln2_g[i]), row2(ln2_b[i]))

    pw = peer_weights(0)
    x = _peer_block(x1, hpl, g2l, *pw)
    ctx = _peer_block(c1.reshape(1, bsz * lc, d), hpc.reshape(1, bsz * lc, d), g2c, *pw).reshape(bsz, lc, d)

    (sh1l, sc1l, g1l, sh2l, sc2l, g2l), (sh1c, sc1c, _, _, _, _) = mod_rows(1)
    qk_w = ML_HEADS * (d // 8)
    v_w = ML_HEADS * (d // 4)
    n_main = 2 * qk_w + 2 * v_w
    w_in = ml_w_in[0]
    wg = jnp.zeros((d, V7X_LANES), F32).at[:, :4 * ML_HEADS].set(w_in[:, n_main:]).astype(BF16)
    bg = jnp.zeros((1, V7X_LANES), F32).at[:, :4 * ML_HEADS].set(ml_b_in[0][n_main:])
    xcat = jnp.concatenate([ctx, x], axis=1)
    q, k, v, o, g = _ml_in(xcat, lc, sh1l, sc1l, sh1c, sc1c, w_in[:, :n_main].astype(BF16),
                           row2(ml_b_in[0][:n_main]), wg, bg, qk_w, v_w)
    s = lc + l
    g2 = g.reshape(bsz, s, 2, 2 * ML_HEADS).transpose(2, 0, 1, 3)
    gt2 = g2.reshape(2, bsz, s // ML_CHUNK, ML_CHUNK, 2 * ML_HEADS).transpose(0, 1, 2, 4, 3)
    h2 = _mlstm(q, k, v, g2, gt2, lc // ML_CHUNK)
    x1, hpl = _ml_out(h2, o, lc, row2(ml_norm_g[0]), ml_w_out[0].astype(BF16), row2(ml_b_out[0]), x, g1l,
                      row2(ln1_g[1]), row2(ln1_b[1]), sh2l, sc2l)
    return _peer_block(x1, hpl, g2l, *peer_weights(1))
```

```python
import functools
import math

import jax
import jax.numpy as jnp
from jax import lax
from jax.experimental import pallas as pl
from jax.experimental.pallas import tpu as pltpu

F32 = jnp.float32
BF16 = jnp.bfloat16

DEPTH = 2
EPS = 1e-6
ALPHA = (2 * DEPTH) ** 0.25
N_MOD = 6
GRID_W = 64

CONV_WIDTH = 31
ATT_HEADS = 8
ATT_KV_HEADS = 2
HEAD_DIM = 64
ROPE_BASE = 10000.0

ML_HEADS = 4
ML_CHUNK = 64

PEER_HEADS = 8
N_KEYS = 128
PEER_TOPK = 16
PEER_A_GROUP = 4

V7X_VMEM_BYTES = 64 * 1024 * 1024
V7X_LANES = 128
V7X_BF16_ROWS = 16
NEG_INF = float("-inf")


def _params(semantics, vmem_bytes):
    return pltpu.CompilerParams(dimension_semantics=semantics,
                                vmem_limit_bytes=min(int(vmem_bytes), V7X_VMEM_BYTES - 8 * 1024 * 1024))


def _ln(x):
    mu = jnp.mean(x, axis=-1, keepdims=True)
    xc = x - mu
    var = jnp.mean(xc * xc, axis=-1, keepdims=True)
    return xc * lax.rsqrt(var + EPS)


def _dot(a, b):
    return jnp.dot(a, b, preferred_element_type=F32)


def _dot_nt(a, b):
    return lax.dot_general(a, b, (((1,), (1,)), ((), ())), preferred_element_type=F32)


def _dot_tn(a, b):
    return lax.dot_general(a, b, (((0,), (0,)), ((), ())), preferred_element_type=F32)


def _mod_kernel(c_ref, w_ref, b_ref, o_ref):
    c = c_ref[...]
    s = (c * jax.nn.sigmoid(c)).astype(BF16)
    o_ref[0] = _dot(s, w_ref[0].astype(BF16)) + b_ref[0]


def _modulation(c_rows, w_mod, b_mod):
    depth, d, n = w_mod.shape
    rows = c_rows.shape[0]
    tn = 1536
    return pl.pallas_call(
        _mod_kernel,
        grid=(depth, n // tn),
        in_specs=[pl.BlockSpec((rows, d), lambda i, j: (0, 0)),
                  pl.BlockSpec((1, d, tn), lambda i, j: (i, 0, j)),
                  pl.BlockSpec((1, 1, tn), lambda i, j: (i, 0, j))],
        out_specs=pl.BlockSpec((1, rows, tn), lambda i, j: (i, 0, j)),
        out_shape=jax.ShapeDtypeStruct((depth, rows, n), F32),
        compiler_params=_params(("parallel", "parallel"), 4 * d * tn * 4),
        name="modulation",
    )(c_rows, w_mod, b_mod.reshape(depth, 1, n))


def _rope(x, cos, sin_signed):
    w = x.shape[-1]
    lane = lax.broadcasted_iota(jnp.int32, x.shape, 1)
    first = (lane % 32) < 16
    partner = jnp.where(first, pltpu.roll(x, w - 16, 1), pltpu.roll(x, 16, 1))
    return x * cos + partner * sin_signed


def _group_rms(x, bd, g):
    w = bd.shape[0]
    parts = []
    for s in range(0, x.shape[-1], w):
        xs = x[:, s:s + w]
        ms = _dot((xs * xs).astype(BF16), bd) * (1.0 / HEAD_DIM)
        parts.append(xs * lax.rsqrt(ms + EPS))
    y = parts[0] if len(parts) == 1 else jnp.concatenate(parts, axis=-1)
    return y * g


def _ab_in_kernel(use_rope, conv_ch, att_w, kv_w, x_ref, sh_ref, sc_ref, w_ref, b_ref, qg_ref, kg_ref, bd_ref,
                  cos_ref, sin_ref, u_ref, q_ref, k_ref, v_ref):
    h = _ln(x_ref[0]) * (1.0 + sc_ref[0]) + sh_ref[0]
    p = _dot(h.astype(BF16), w_ref[...]) + b_ref[...]
    c1, c2, c3, c4 = conv_ch, 2 * conv_ch, 2 * conv_ch + att_w, 2 * conv_ch + att_w + kv_w
    u_ref[0] = p[:, :c1] * jax.nn.sigmoid(p[:, c1:c2])
    q = _group_rms(p[:, c2:c3], bd_ref[...], qg_ref[...])
    k = _group_rms(p[:, c3:c4], bd_ref[:kv_w, :kv_w], kg_ref[...])
    if use_rope:
        q = _rope(q, cos_ref[...], sin_ref[...])
        k = _rope(k, cos_ref[:, :kv_w], sin_ref[:, :kv_w])
    q_ref[0] = (q * (HEAD_DIM ** -0.5)).astype(BF16)
    k_ref[0] = k.astype(BF16)
    v_ref[0] = p[:, c4:].astype(BF16)


def _ab_in(x, sh, sc, w, b, qg, kg, bd, cos, sin, use_rope, conv_ch, att_w, kv_w):
    bn, l, d = x.shape
    n = w.shape[1]
    tm = min(512, l)
    kern = functools.partial(_ab_in_kernel, use_rope, conv_ch, att_w, kv_w)
    row = lambda i, j: (i, j, 0)
    mod = lambda i, j: (i, 0, 0)
    fix = lambda i, j: (0, 0)
    return pl.pallas_call(
        kern,
        grid=(bn, l // tm),
        in_specs=[pl.BlockSpec((1, tm, d), row), pl.BlockSpec((1, 1, d), mod), pl.BlockSpec((1, 1, d), mod),
                  pl.BlockSpec((d, n), fix), pl.BlockSpec((1, n), fix),
                  pl.BlockSpec((1, att_w), fix), pl.BlockSpec((1, kv_w), fix), pl.BlockSpec(bd.shape, fix),
                  pl.BlockSpec((tm, att_w), lambda i, j: (j, 0)), pl.BlockSpec((tm, att_w), lambda i, j: (j, 0))],
        out_specs=[pl.BlockSpec((1, tm, conv_ch), row), pl.BlockSpec((1, tm, att_w), row),
                   pl.BlockSpec((1, tm, kv_w), row), pl.BlockSpec((1, tm, kv_w), row)],
        out_shape=[jax.ShapeDtypeStruct((bn, l, conv_ch), F32), jax.ShapeDtypeStruct((bn, l, att_w), BF16),
                   jax.ShapeDtypeStruct((bn, l, kv_w), BF16), jax.ShapeDtypeStruct((bn, l, kv_w), BF16)],
        compiler_params=_params(("parallel", "parallel"), 2 * (tm * d * 4 + d * n * 2) + 8 * tm * n * 4),
        name="ab_in_proj",
    )(x, sh, sc, w, b, qg, kg, bd, cos, sin)


def _attn_kernel(q_ref, k_ref, v_ref, o_ref):
    group = ATT_HEADS // ATT_KV_HEADS
    for kh in range(ATT_KV_HEADS):
        kk = k_ref[0, :, kh * HEAD_DIM:(kh + 1) * HEAD_DIM]
        vv = v_ref[0, :, kh * HEAD_DIM:(kh + 1) * HEAD_DIM]
        for g in range(group):
            lo = (kh * group + g) * HEAD_DIM
            s = _dot_nt(q_ref[0, :, lo:lo + HEAD_DIM], kk)
            p = jnp.exp(s - jnp.max(s, axis=-1, keepdims=True))
            denom = jnp.sum(p, axis=-1, keepdims=True)
            o_ref[0, :, lo:lo + HEAD_DIM] = _dot(p.astype(BF16), vv) / denom


def _attention(q, k, v):
    bn, lq, w = q.shape
    s, kvw = k.shape[1], k.shape[2]
    tq = min(256, lq)
    return pl.pallas_call(
        _attn_kernel,
        grid=(bn, lq // tq),
        in_specs=[pl.BlockSpec((1, tq, w), lambda i, j: (i, j, 0)),
                  pl.BlockSpec((1, s, kvw), lambda i, j: (i, 0, 0)),
                  pl.BlockSpec((1, s, kvw), lambda i, j: (i, 0, 0))],
        out_specs=pl.BlockSpec((1, tq, w), lambda i, j: (i, j, 0)),
        out_shape=jax.ShapeDtypeStruct((bn, lq, w), F32),
        compiler_params=_params(("parallel", "parallel"), 8 * tq * s * 4 + 8 * s * kvw * 2),
        name="gqa_attention",
    )(q, k, v)


CONV_ROWS = 32
CONV_HALO = 16


def _conv_kernel(u_ref, w_ref, cb_ref, g_ref, b_ref, o_ref, pad_ref):
    l, ch = u_ref.shape[1], u_ref.shape[2]
    zeros = jnp.zeros((CONV_HALO, ch), F32)
    pad_ref[0:CONV_HALO, :] = zeros
    pad_ref[CONV_HALO + l:2 * CONV_HALO + l, :] = zeros
    pad_ref[CONV_HALO:CONV_HALO + l, :] = u_ref[0]
    first = CONV_HALO - CONV_WIDTH // 2

    def tile(r, carry):
        base = pl.multiple_of(r * CONV_ROWS, CONV_ROWS)
        acc = jnp.zeros((CONV_ROWS, ch), F32)
        win = pad_ref[pl.ds(base, CONV_ROWS + 2 * CONV_HALO), :]
        for j in range(CONV_WIDTH):
            acc = acc + w_ref[j:j + 1, :] * win[first + j:first + j + CONV_ROWS, :]
        y = _ln(acc + cb_ref[...]) * g_ref[...] + b_ref[...]
        o_ref[0, pl.ds(base, CONV_ROWS), :] = y * jax.nn.sigmoid(y)
        return carry

    lax.fori_loop(0, l // CONV_ROWS, tile, 0)


def _conv_group(u, w, cb, g, b):
    bn, l, ch = u.shape
    fix = lambda i: (0, 0)
    return pl.pallas_call(
        _conv_kernel,
        grid=(bn,),
        in_specs=[pl.BlockSpec((1, l, ch), lambda i: (i, 0, 0)), pl.BlockSpec((CONV_WIDTH, ch), fix),
                  pl.BlockSpec((1, ch), fix), pl.BlockSpec((1, ch), fix), pl.BlockSpec((1, ch), fix)],
        out_specs=pl.BlockSpec((1, l, ch), lambda i: (i, 0, 0)),
        out_shape=jax.ShapeDtypeStruct((bn, l, ch), F32),
        scratch_shapes=[pltpu.VMEM((l + 2 * CONV_HALO, ch), F32)],
        compiler_params=_params(("parallel",), 6 * l * ch * 4),
        name="conv_group",
    )(u, w, cb, g, b)


def _resid_epilogue(y, x_ref, g1_ref, lng_ref, lnb_ref, sh2_ref, sc2_ref, x1_ref, hp_ref):
    x1 = _ln(ALPHA * x_ref[0] + g1_ref[0] * y) * lng_ref[...] + lnb_ref[...]
    x1_ref[0] = x1
    hp_ref[0] = (_ln(x1) * (1.0 + sc2_ref[0]) + sh2_ref[0]).astype(BF16)


def _ab_out_kernel(a_ref, t_ref, w1_ref, w2_ref, b_ref, x_ref, g1_ref, lng_ref, lnb_ref, sh2_ref, sc2_ref,
                   x1_ref, hp_ref):
    y = _dot(a_ref[0].astype(BF16), w1_ref[...]) + _dot(t_ref[0].astype(BF16), w2_ref[...]) + b_ref[...]
    _resid_epilogue(y, x_ref, g1_ref, lng_ref, lnb_ref, sh2_ref, sc2_ref, x1_ref, hp_ref)


def _ab_out(conv, att, w1, w2, b, x, g1, lng, lnb, sh2, sc2):
    bn, l, d = x.shape
    tm = min(512, l)
    row = lambda i, j: (i, j, 0)
    mod = lambda i, j: (i, 0, 0)
    fix = lambda i, j: (0, 0)
    k1, k2 = conv.shape[2], att.shape[2]
    return pl.pallas_call(
        _ab_out_kernel,
        grid=(bn, l // tm),
        in_specs=[pl.BlockSpec((1, tm, k1), row), pl.BlockSpec((1, tm, k2), row),
                  pl.BlockSpec((k1, d), fix), pl.BlockSpec((k2, d), fix), pl.BlockSpec((1, d), fix),
                  pl.BlockSpec((1, tm, d), row), pl.BlockSpec((1, 1, d), mod),
                  pl.BlockSpec((1, d), fix), pl.BlockSpec((1, d), fix),
                  pl.BlockSpec((1, 1, d), mod), pl.BlockSpec((1, 1, d), mod)],
        out_specs=[pl.BlockSpec((1, tm, d), row), pl.BlockSpec((1, tm, d), row)],
        out_shape=[jax.ShapeDtypeStruct((bn, l, d), F32), jax.ShapeDtypeStruct((bn, l, d), BF16)],
        compiler_params=_params(("parallel", "parallel"), 16 * tm * d * 4),
        name="ab_out_proj",
    )(conv, att, w1, w2, b, x, g1, lng, lnb, sh2, sc2)


def _ml_out_kernel(hf_ref, hb_ref, o_ref, ng_ref, w_ref, b_ref, x_ref, g1_ref, lng_ref, lnb_ref, sh2_ref, sc2_ref,
                   x1_ref, hp_ref):
    dv = hf_ref.shape[2] // ML_HEADS
    hs = hf_ref[0] + hb_ref[0]
    parts = [_ln(hs[:, h * dv:(h + 1) * dv]) for h in range(ML_HEADS)]
    hn = jnp.concatenate(parts, axis=-1) * ng_ref[...]
    y = _dot((hn * jax.nn.sigmoid(o_ref[0])).astype(BF16), w_ref[...]) + b_ref[...]
    _resid_epilogue(y, x_ref, g1_ref, lng_ref, lnb_ref, sh2_ref, sc2_ref, x1_ref, hp_ref)


def _ml_out(hf, hb, o, seq_off, ng, w, b, x, g1, lng, lnb, sh2, sc2):
    bn, l, d = x.shape
    vw = o.shape[2]
    tm = math.gcd(256, math.gcd(l, seq_off))
    off = seq_off // tm
    row = lambda i, j: (i, j, 0)
    mod = lambda i, j: (i, 0, 0)
    fix = lambda i, j: (0, 0)
    return pl.pallas_call(
        _ml_out_kernel,
        grid=(bn, l // tm),
        in_specs=[pl.BlockSpec((1, tm, vw), lambda i, j: (i, j + off, 0)),
                  pl.BlockSpec((1, tm, vw), lambda i, j: (i, j + off, 0)),
                  pl.BlockSpec((1, tm, vw), lambda i, j: (i, j + off, 0)),
                  pl.BlockSpec((1, vw), fix), pl.BlockSpec((vw, d), fix), pl.BlockSpec((1, d), fix),
                  pl.BlockSpec((1, tm, d), row), pl.BlockSpec((1, 1, d), mod),
                  pl.BlockSpec((1, d), fix), pl.BlockSpec((1, d), fix),
                  pl.BlockSpec((1, 1, d), mod), pl.BlockSpec((1, 1, d), mod)],
        out_specs=[pl.BlockSpec((1, tm, d), row), pl.BlockSpec((1, tm, d), row)],
        out_shape=[jax.ShapeDtypeStruct((bn, l, d), F32), jax.ShapeDtypeStruct((bn, l, d), BF16)],
        compiler_params=_params(("parallel", "parallel"), 24 * tm * d * 4),
        name="ml_out_proj",
    )(hf, hb, o, ng, w, b, x, g1, lng, lnb, sh2, sc2)


def _ml_in_kernel(qk_w, v_w, x_ref, shl_ref, scl_ref, shc_ref, scc_ref, w_ref, b_ref, wg_ref, bg_ref,
                  q_ref, k_ref, v_ref, o_ref, g_ref):
    is_ctx = pl.program_id(1) == 0
    sh = jnp.where(is_ctx, shc_ref[0], shl_ref[0])
    sc = jnp.where(is_ctx, scc_ref[0], scl_ref[0])
    h = (_ln(x_ref[0]) * (1.0 + sc) + sh).astype(BF16)
    p = _dot(h, w_ref[...]) + b_ref[...]
    dqk = qk_w // ML_HEADS
    q_ref[0] = p[:, :qk_w].astype(BF16)
    k_ref[0] = (p[:, qk_w:2 * qk_w] * (dqk ** -0.5)).astype(BF16)
    v_ref[0] = p[:, 2 * qk_w:2 * qk_w + v_w].astype(BF16)
    o_ref[0] = p[:, 2 * qk_w + v_w:]
    g_ref[0] = (_dot(h, wg_ref[...]) + bg_ref[...])[:, :g_ref.shape[2]]


def _ml_in(xcat, lc, shl, scl, shc, scc, w, b, wg, bg, qk_w, v_w):
    bn, s, d = xcat.shape
    n = w.shape[1]
    ng = 4 * ML_HEADS
    tm = lc
    row = lambda i, j: (i, j, 0)
    mod = lambda i, j: (i, 0, 0)
    one = lambda i, j: (0, 0, 0)
    fix = lambda i, j: (0, 0)
    kern = functools.partial(_ml_in_kernel, qk_w, v_w)
    return pl.pallas_call(
        kern,
        grid=(bn, s // tm),
        in_specs=[pl.BlockSpec((1, tm, d), row), pl.BlockSpec((1, 1, d), mod), pl.BlockSpec((1, 1, d), mod),
                  pl.BlockSpec((1, 1, d), one), pl.BlockSpec((1, 1, d), one),
                  pl.BlockSpec((d, n), fix), pl.BlockSpec((1, n), fix),
                  pl.BlockSpec(wg.shape, fix), pl.BlockSpec(bg.shape, fix)],
        out_specs=[pl.BlockSpec((1, tm, qk_w), row), pl.BlockSpec((1, tm, qk_w), row),
                   pl.BlockSpec((1, tm, v_w), row), pl.BlockSpec((1, tm, v_w), row),
                   pl.BlockSpec((1, tm, ng), row)],
        out_shape=[jax.ShapeDtypeStruct((bn, s, qk_w), BF16), jax.ShapeDtypeStruct((bn, s, qk_w), BF16),
                   jax.ShapeDtypeStruct((bn, s, v_w), BF16), jax.ShapeDtypeStruct((bn, s, v_w), F32),
                   jax.ShapeDtypeStruct((bn, s, ng), F32)],
        compiler_params=_params(("parallel", "parallel"), 2 * (tm * d * 4 + d * n * 2) + 8 * tm * n * 4),
        name="ml_in_proj",
    )(xcat, shl, scl, shc, scc, w, b, wg, bg)


def _log_sigmoid(x):
    return jnp.minimum(x, 0.0) - jnp.log1p(jnp.exp(-jnp.abs(x)))


def _mlstm_kernel(qf_ref, kf_ref, vf_ref, gf_ref, gtf_ref, qb_ref, kb_ref, vb_ref, gb_ref, gtb_ref,
                  hf_ref, hb_ref, ct_ref, n_ref, m_ref):
    @pl.when(pl.program_id(1) == 0)
    def _():
        ct_ref[...] = jnp.zeros_like(ct_ref)
        n_ref[...] = jnp.zeros_like(n_ref)
        m_ref[...] = jnp.zeros_like(m_ref)

    ch = qf_ref.shape[1]
    dk = qf_ref.shape[2] // ML_HEADS
    dv = vf_ref.shape[2] // ML_HEADS
    row = lax.broadcasted_iota(jnp.int32, (ch, ch), 0)
    col = lax.broadcasted_iota(jnp.int32, (ch, ch), 1)
    chains = []
    for backward, q_ref, k_ref, v_ref, g_ref, gt_ref, h_ref in (
            (False, qf_ref, kf_ref, vf_ref, gf_ref, gtf_ref, hf_ref),
            (True, qb_ref, kb_ref, vb_ref, gb_ref, gtb_ref, hb_ref)):
        seen = (col >= row) if backward else (col <= row)
        g = g_ref[0, 0]
        gt = gt_ref[0, 0, 0]
        f_cols = _log_sigmoid(g[:, ML_HEADS:])
        f_rows = _log_sigmoid(gt[ML_HEADS:, :])
        seen_f = seen.astype(F32)
        b_cols = jnp.dot(seen_f, f_cols, precision=lax.Precision.HIGHEST, preferred_element_type=F32)
        b_rows = lax.dot_general(f_rows, seen_f, (((1,), (1,)), ((), ())), precision=lax.Precision.HIGHEST,
                                 preferred_element_type=F32)
        for head in range(ML_HEADS):
            chains.append(dict(
                st=(ML_HEADS if backward else 0) + head, head=head, seen=seen, h_ref=h_ref,
                i_col=g[:, head:head + 1], i_row=gt[head:head + 1, :], f_row=f_rows[head:head + 1, :],
                b_col=b_cols[:, head:head + 1], b_row=b_rows[head:head + 1, :],
                q=q_ref[0, :, head * dk:(head + 1) * dk], k=k_ref[0, :, head * dk:(head + 1) * dk],
                v=v_ref[0, :, head * dv:(head + 1) * dv]))
    for c in chains:
        c["m"] = m_ref[c["st"]]
        c["a_col"] = c["b_col"] + c["m"]
        c["dlog"] = jnp.where(c["seen"], c["b_col"] - c["b_row"] + c["i_row"], NEG_INF)
        c["qk"] = _dot_nt(c["q"], c["k"])
        c["ct"] = ct_ref[c["st"]]
        c["qc"] = _dot(c["q"], c["ct"].astype(BF16))
    for c in chains:
        c["mt"] = jnp.maximum(c["a_col"], jnp.max(c["dlog"], axis=1, keepdims=True))
        c["qn"] = jnp.sum(c["q"].astype(F32) * n_ref[c["st"]], axis=1, keepdims=True)
        c["bl"] = jnp.sum(c["f_row"], axis=1, keepdims=True)
        c["wl"] = c["bl"] - c["b_col"] + c["i_col"]
    for c in chains:
        c["smat"] = c["qk"] * jnp.exp(c["dlog"] - c["mt"])
        c["aw"] = jnp.exp(c["a_col"] - c["mt"])
        c["mn"] = jnp.maximum(c["bl"] + c["m"], jnp.max(c["wl"], axis=0, keepdims=True))
    for c in chains:
        c["sv"] = _dot(c["smat"].astype(BF16), c["v"])
        c["den"] = jnp.sum(c["smat"], axis=1, keepdims=True) + c["aw"] * c["qn"]
        wc = jnp.exp(c["wl"] - c["mn"])
        c["wc"] = wc
        c["kv"] = _dot_tn(c["k"], (wc * c["v"].astype(F32)).astype(BF16))
    for c in chains:
        head, st = c["head"], c["st"]
        num = c["sv"] + c["aw"] * c["qc"]
        c["h_ref"][0, :, head * dv:(head + 1) * dv] = num / jnp.maximum(jnp.abs(c["den"]), jnp.exp(-c["mt"]))
        dc = jnp.exp(c["bl"] + c["m"] - c["mn"])
        ct_ref[st] = dc * c["ct"] + c["kv"]
        n_ref[st] = dc * n_ref[st] + jnp.sum(c["wc"] * c["k"].astype(F32), axis=0, keepdims=True)
        m_ref[st] = c["mn"]


def _mlstm(q, k, v, g2, gt2, n_ctx_chunks):
    bn, s, qk_w = q.shape
    v_w = v.shape[2]
    ch = ML_CHUNK
    nc = s // ch
    ncx = n_ctx_chunks

    def back(t):
        return jnp.where(t < ncx, ncx - 1 - t, nc + ncx - 1 - t)

    def specs(d, chunk):
        seq = lambda b, t: (b, chunk(t), 0)
        return [pl.BlockSpec((1, ch, qk_w), seq), pl.BlockSpec((1, ch, qk_w), seq), pl.BlockSpec((1, ch, v_w), seq),
                pl.BlockSpec((1, 1, ch, 2 * ML_HEADS), lambda b, t: (d, b, chunk(t), 0)),
                pl.BlockSpec((1, 1, 1, 2 * ML_HEADS, ch), lambda b, t: (d, b, chunk(t), 0, 0))]

    fwd = lambda t: t
    h_shape = jax.ShapeDtypeStruct((bn, s, v_w), F32)
    states = 2 * ML_HEADS
    return pl.pallas_call(
        _mlstm_kernel,
        grid=(bn, nc),
        in_specs=specs(0, fwd) + specs(1, back),
        out_specs=[pl.BlockSpec((1, ch, v_w), lambda b, t: (b, t, 0)),
                   pl.BlockSpec((1, ch, v_w), lambda b, t: (b, back(t), 0))],
        out_shape=[h_shape, h_shape],
        scratch_shapes=[pltpu.VMEM((states, qk_w // ML_HEADS, v_w // ML_HEADS), F32),
                        pltpu.VMEM((states, 1, qk_w // ML_HEADS), F32),
                        pltpu.VMEM((states, 1, 1), F32)],
        compiler_params=_params(("parallel", "arbitrary"), 16 * 1024 * 1024),
        name="mlstm_scan",
    )(q, k, v, g2, gt2, q, k, v, g2, gt2)


def _top_keys(s, vals_ref):
    rank = jnp.full(s.shape, float(PEER_TOPK), F32)
    for i in range(PEER_TOPK):
        m = jnp.max(s, axis=0, keepdims=True)
        hit = s == m
        rank = jnp.where(hit, float(i), rank)
        s = jnp.where(hit, NEG_INF, s)
        vals_ref[i:i + 1, :] = m
    return rank


def _peer_sel_kernel(hp_ref, wqt_ref, k1_ref, k2_ref, r2_ref, e2_ref, kk_ref, cc_ref, v1_ref, v2_ref, cnt_ref):
    half = N_KEYS
    qt = _dot_nt(wqt_ref[...], hp_ref[...])
    for h in range(PEER_HEADS):
        q1 = qt[(2 * h) * half:(2 * h + 1) * half].astype(BF16)
        q2 = qt[(2 * h + 1) * half:(2 * h + 2) * half].astype(BF16)
        s1 = _dot(k1_ref[h], q1)
        s2 = _dot(k2_ref[h], q2)
        r1 = _top_keys(s1, v1_ref)
        r2 = _top_keys(s2, v2_ref)
        v1 = v1_ref[...]
        v2 = v2_ref[...]
        cand = jnp.concatenate(
            [v1 + v2[0:1]] + [v1[0:8] + v2[j:j + 1] for j in range(1, 8)] + [v1[0:1] + v2[8:16]], axis=0)
        work = cand
        theta = None
        for _ in range(PEER_TOPK):
            theta = jnp.max(work, axis=0, keepdims=True)
            work = jnp.where(work == theta, NEG_INF, work)
        smax = v1[0:1] + v2[0:1]
        sel = cand >= theta
        z = jnp.sum(jnp.where(sel, jnp.exp(cand - smax), 0.0), axis=0, keepdims=True)
        picked = sel.astype(F32)
        cnt_ref[...] = picked[0:16]
        cnt_ref[0:8, :] += functools.reduce(jnp.add, [picked[8 + 8 * j:16 + 8 * j] for j in range(1, 8)])
        cnt_ref[0:1, :] += jnp.sum(picked[72:80], axis=0, keepdims=True)
        kk = jnp.zeros(s1.shape, F32)
        for i in range(PEER_TOPK):
            kk = jnp.where(r1 == float(i), cnt_ref[i:i + 1, :], kk)
        r2 = r2.astype(BF16)
        e2 = jnp.exp(s2 - v2[0:1]).astype(BF16)
        for grp in range(N_KEYS // V7X_BF16_ROWS):
            r2_ref[0, h, grp] = r2[grp * V7X_BF16_ROWS:(grp + 1) * V7X_BF16_ROWS]
            e2_ref[0, h, grp] = e2[grp * V7X_BF16_ROWS:(grp + 1) * V7X_BF16_ROWS]
        kk_ref[0, h] = kk.astype(BF16)
        cc_ref[0, h] = (jnp.exp(s1 - v1[0:1]) / z).astype(BF16)


def _peer_select(hp, wqt, k1, k2, t):
    nt, d = hp.shape
    nb = nt // t
    sel_spec = pl.BlockSpec((1, PEER_HEADS, N_KEYS, t), lambda i: (i, 0, 0, 0))
    sel_shape = jax.ShapeDtypeStruct((nb, PEER_HEADS, N_KEYS, t), BF16)
    groups = N_KEYS // V7X_BF16_ROWS
    pk_spec = pl.BlockSpec((1, PEER_HEADS, groups, V7X_BF16_ROWS, t), lambda i: (i, 0, 0, 0, 0))
    pk_shape = jax.ShapeDtypeStruct((nb, PEER_HEADS, groups, V7X_BF16_ROWS, t), BF16)
    return pl.pallas_call(
        _peer_sel_kernel,
        grid=(nb,),
        in_specs=[pl.BlockSpec((t, d), lambda i: (i, 0)), pl.BlockSpec(wqt.shape, lambda i: (0, 0)),
                  pl.BlockSpec(k1.shape, lambda i: (0, 0, 0)), pl.BlockSpec(k2.shape, lambda i: (0, 0, 0))],
        out_specs=[pk_spec, pk_spec, sel_spec, sel_spec],
        out_shape=[pk_shape, pk_shape, sel_shape, sel_shape],
        scratch_shapes=[pltpu.VMEM((PEER_TOPK, t), F32), pltpu.VMEM((PEER_TOPK, t), F32),
                        pltpu.VMEM((PEER_TOPK, t), F32)],
        compiler_params=_params(("parallel",), 40 * 1024 * 1024),
        name="peer_select",
    )(hp, wqt, k1, k2)


def _gelu(x):
    return 0.5 * x * (1.0 + lax.erf(x * (2.0 ** -0.5)))


def _peer_dense_kernel(hp_ref, u_ref, vt_ref, r2_ref, e2_ref, kk_ref, cc_ref, x1_ref, g2_ref, lng_ref, lnb_ref,
                       x2_ref, acc_ref, w_ref, r2s_ref, e2s_ref, act_ref):
    j = pl.program_id(1)
    tn, t = u_ref.shape[0], hp_ref.shape[0]
    na = tn // N_KEYS

    @pl.when(j == 0)
    def _():
        acc_ref[...] = jnp.zeros_like(acc_ref)
        r2s_ref[...] = r2_ref[0]
        e2s_ref[...] = e2_ref[0]

    a_rows = pl.ds(pl.multiple_of(j * na, na), na)
    tile = (N_KEYS // V7X_BF16_ROWS, V7X_BF16_ROWS, V7X_LANES)
    for ag in range(0, na, PEER_A_GROUP):
        rows = slice(ag * N_KEYS, (ag + PEER_A_GROUP) * N_KEYS)
        act_ref[...] = _gelu(_dot_nt(u_ref[rows, :], hp_ref[...])).astype(BF16).reshape(act_ref.shape)

        for lt in range(t // V7X_LANES):
            lanes = slice(lt * V7X_LANES, (lt + 1) * V7X_LANES)
            gsum = [jnp.zeros(tile, BF16) for _ in range(PEER_A_GROUP)]
            for h in range(PEER_HEADS):
                r2 = r2s_ref[h, :, :, lanes]
                e2 = e2s_ref[h, :, :, lanes]
                kk = kk_ref[0, h, a_rows, lanes].astype(F32)
                cc = cc_ref[0, h, a_rows, lanes].astype(F32)
                for i in range(PEER_A_GROUP):
                    al = ag + i
                    kk_t = jnp.broadcast_to(kk[al:al + 1], tile[1:]).astype(BF16)[None]
                    cc_t = jnp.broadcast_to(cc[al:al + 1], tile[1:]).astype(BF16)[None]
                    gsum[i] = jnp.where(r2 < kk_t, gsum[i] + e2 * cc_t, gsum[i])
            for i in range(PEER_A_GROUP):
                w_ref[(ag + i) * tile[0]:(ag + i + 1) * tile[0], :, lanes] = (
                    gsum[i] * act_ref[i * tile[0]:(i + 1) * tile[0], :, lanes])
    acc_ref[...] += _dot(vt_ref[...], w_ref[...].reshape(tn, t))

    @pl.when(j == pl.num_programs(1) - 1)
    def _():
        y = ALPHA * x1_ref[...] + g2_ref[0] * acc_ref[...].T
        x2_ref[...] = _ln(y) * lng_ref[...] + lnb_ref[...]


def _peer_dense(hp, u, vt, sel, x1, g2, blocks_per_row, lng, lnb, t, tn):
    nt, d = hp.shape
    ne = u.shape[0]
    sel_spec = pl.BlockSpec((1, PEER_HEADS, N_KEYS, t), lambda i, j: (i, 0, 0, 0))
    pk_spec = pl.BlockSpec((1, PEER_HEADS, N_KEYS // V7X_BF16_ROWS, V7X_BF16_ROWS, t), lambda i, j: (i, 0, 0, 0, 0))
    fix = lambda i, j: (0, 0)
    return pl.pallas_call(
        _peer_dense_kernel,
        grid=(nt // t, ne // tn),
        in_specs=[pl.BlockSpec((t, d), lambda i, j: (i, 0)), pl.BlockSpec((tn, d), lambda i, j: (j, 0)),
                  pl.BlockSpec((d, tn), lambda i, j: (0, j)), pk_spec, pk_spec, sel_spec, sel_spec,
                  pl.BlockSpec((t, d), lambda i, j: (i, 0)),
                  pl.BlockSpec((1, 1, d), lambda i, j: (i // blocks_per_row, 0, 0)),
                  pl.BlockSpec((1, d), fix), pl.BlockSpec((1, d), fix)],
        out_specs=pl.BlockSpec((t, d), lambda i, j: (i, 0)),
        out_shape=jax.ShapeDtypeStruct((nt, d), F32),
        scratch_shapes=[pltpu.VMEM((d, t), F32), pltpu.VMEM((tn // V7X_BF16_ROWS, V7X_BF16_ROWS, t), BF16),
                        pltpu.VMEM(pk_spec.block_shape[1:], BF16), pltpu.VMEM(pk_spec.block_shape[1:], BF16),
                        pltpu.VMEM((PEER_A_GROUP * N_KEYS // V7X_BF16_ROWS, V7X_BF16_ROWS, t), BF16)],
        compiler_params=_params(("parallel", "arbitrary"), 48 * 1024 * 1024),
        name="peer_dense",
    )(hp, u, vt, *sel, x1, g2, lng, lnb)


def _peer_block(x1, hp, g2, wqt, k1, k2, u, vt, lng, lnb):
    bn, l, d = x1.shape
    t = min(512, l)
    tn = 16 * N_KEYS
    hp2 = hp.reshape(bn * l, d)
    sel = _peer_select(hp2, wqt, k1, k2, t)
    x2 = _peer_dense(hp2, u, vt, sel, x1.reshape(bn * l, d), g2, l // t, lng, lnb, t, tn)
    return x2.reshape(bn, l, d)


def _rope_tables(l, att_w):
    rows = l // GRID_W
    row = jnp.repeat(jnp.arange(rows, dtype=F32), GRID_W)
    col = jnp.tile(jnp.arange(GRID_W, dtype=F32), rows)
    f = HEAD_DIM // 4
    inv_freq = ROPE_BASE ** (-jnp.arange(f, dtype=F32) / f)
    ar = row[:, None] * inv_freq[None, :]
    ac = col[:, None] * inv_freq[None, :]
    cos = jnp.concatenate([jnp.cos(ar), jnp.cos(ar), jnp.cos(ac), jnp.cos(ac)], axis=-1)
    sin = jnp.concatenate([-jnp.sin(ar), jnp.sin(ar), -jnp.sin(ac), jnp.sin(ac)], axis=-1)
    reps = att_w // HEAD_DIM
    return jnp.tile(cos, (1, reps)), jnp.tile(sin, (1, reps))


def _block_diag_ones(width, group):
    idx = jnp.arange(width) // group
    return (idx[:, None] == idx[None, :]).astype(BF16)


def kernel(x, c, ctx, c_ctx, w_mod, b_mod, ln1_g, ln1_b, ln2_g, ln2_b, ab_w_in, ab_b_in, ab_conv_w, ab_conv_b,
           ab_conv_ln_g, ab_conv_ln_b, ab_q_norm_g, ab_k_norm_g, ab_w_out, ab_b_out, ml_w_in, ml_b_in, ml_norm_g,
           ml_w_out, ml_b_out, peer_w_q, peer_k1, peer_k2, peer_u, peer_v):
    bsz, l, d = x.shape
    lc = ctx.shape[1]
    assert l % GRID_W == 0 and lc % ML_CHUNK == 0 and l % lc == 0

    rows = -(-(bsz + 1) // 8) * 8
    c_rows = jnp.zeros((rows, d), F32).at[:bsz].set(c).at[bsz].set(c_ctx)
    mod = _modulation(c_rows, w_mod, b_mod)

    def mod_rows(i):
        m = mod[i].reshape(rows, N_MOD, d)
        lat = [m[:bsz, k][:, None, :] for k in range(N_MOD)]
        cx = [m[bsz:bsz + 1, k][:, None, :] for k in range(N_MOD)]
        return lat, cx

    row2 = lambda v: v.reshape(1, -1)

    conv_ch = ab_conv_w.shape[2]
    kv_w = ATT_KV_HEADS * HEAD_DIM
    att_w = ATT_HEADS * HEAD_DIM
    (sh1l, sc1l, g1l, sh2l, sc2l, g2l), (sh1c, sc1c, g1c, sh2c, sc2c, g2c) = mod_rows(0)
    bcast = lambda v: jnp.broadcast_to(v, (bsz, 1, d))
    w_in = ab_w_in[0].astype(BF16)
    b_in = row2(ab_b_in[0])
    qg = row2(jnp.tile(ab_q_norm_g[0], ATT_HEADS))
    kg = row2(jnp.tile(ab_k_norm_g[0], ATT_KV_HEADS))
    bd = _block_diag_ones(2 * V7X_LANES, HEAD_DIM)
    cos, sin = _rope_tables(l, att_w)
    ul, ql, kl, vl = _ab_in(x, sh1l, sc1l, w_in, b_in, qg, kg, bd, cos, sin, True, conv_ch, att_w, kv_w)
    uc, qc, kc, vc = _ab_in(ctx, bcast(sh1c), bcast(sc1c), w_in, b_in, qg, kg, bd, cos[:lc], sin[:lc], False,
                            conv_ch, att_w, kv_w)
    conv_args = (ab_conv_w[0], row2(ab_conv_b[0]), row2(ab_conv_ln_g[0]), row2(ab_conv_ln_b[0]))
    conv_l = _conv_group(ul, *conv_args)
    conv_c = _conv_group(uc, *conv_args)
    att_l = _attention(ql, jnp.concatenate([kc, kl], axis=1), jnp.concatenate([vc, vl], axis=1))
    att_c = _attention(qc, kc, vc)
    w_out = ab_w_out[0].astype(BF16)
    out_args = (w_out[:conv_ch], w_out[conv_ch:], row2(ab_b_out[0]))
    ln1 = (row2(ln1_g[0]), row2(ln1_b[0]))
    x1, hpl = _ab_out(conv_l, att_l, *out_args, x, g1l, *ln1, sh2l, sc2l)
    c1, hpc = _ab_out(conv_c, att_c, *out_args, ctx, bcast(g1c), *ln1, bcast(sh2c), bcast(sc2c))

    def peer_weights(i):
        wqt = peer_w_q[i].T.astype(BF16)
        half = peer_k1.shape[3]
        assert half == N_KEYS
        return (wqt, peer_k1[i].astype(BF16), peer_k2[i].astype(BF16), peer_u[i].astype(BF16),
                peer_v[i].T.astype(BF16), row2(ln2_g[i]), row2(ln2_b[i]))

    pw = peer_weights(0)
    x = _peer_block(x1, hpl, g2l, *pw)
    ctx = _peer_block(c1.reshape(1, bsz * lc, d), hpc.reshape(1, bsz * lc, d), g2c, *pw).reshape(bsz, lc, d)

    (sh1l, sc1l, g1l, sh2l, sc2l, g2l), (sh1c, sc1c, _, _, _, _) = mod_rows(1)
    qk_w = ML_HEADS * (d // 8)
    v_w = ML_HEADS * (d // 4)
    n_main = 2 * qk_w + 2 * v_w
    w_in = ml_w_in[0]
    wg = jnp.zeros((d, V7X_LANES), F32).at[:, :4 * ML_HEADS].set(w_in[:, n_main:]).astype(BF16)
    bg = jnp.zeros((1, V7X_LANES), F32).at[:, :4 * ML_HEADS].set(ml_b_in[0][n_main:])
    xcat = jnp.concatenate([ctx, x], axis=1)
    q, k, v, o, g = _ml_in(xcat, lc, sh1l, sc1l, sh1c, sc1c, w_in[:, :n_main].astype(BF16),
                           row2(ml_b_in[0][:n_main]), wg, bg, qk_w, v_w)
    s = lc + l
    g2 = g.reshape(bsz, s, 2, 2 * ML_HEADS).transpose(2, 0, 1, 3)
    gt2 = g2.reshape(2, bsz, s // ML_CHUNK, ML_CHUNK, 2 * ML_HEADS).transpose(0, 1, 2, 4, 3)
    hf, hb = _mlstm(q, k, v, g2, gt2, lc // ML_CHUNK)
    x1, hpl = _ml_out(hf, hb, o, lc, row2(ml_norm_g[0]), ml_w_out[0].astype(BF16), row2(ml_b_out[0]), x, g1l,
                      row2(ln1_g[1]), row2(ln1_b[1]), sh2l, sc2l)
    return _peer_block(x1, hpl, g2l, *peer_weights(1))
```

```python
import functools
import math

import jax
import jax.numpy as jnp
from jax import lax
from jax.experimental import pallas as pl
from jax.experimental.pallas import tpu as pltpu

F32 = jnp.float32
BF16 = jnp.bfloat16

DEPTH = 2
EPS = 1e-6
ALPHA = (2 * DEPTH) ** 0.25
N_MOD = 6
GRID_W = 64

CONV_WIDTH = 31
ATT_HEADS = 8
ATT_KV_HEADS = 2
HEAD_DIM = 64
ROPE_BASE = 10000.0

ML_HEADS = 4
ML_CHUNK = 64

PEER_HEADS = 8
N_KEYS = 128
PEER_TOPK = 16
PEER_A_GROUP = 4

V7X_VMEM_BYTES = 64 * 1024 * 1024
V7X_LANES = 128
V7X_SUBLANES = 8
V7X_BF16_ROWS = 16
NEG_INF = float("-inf")


def _params(semantics, vmem_bytes):
    return pltpu.CompilerParams(dimension_semantics=semantics,
                                vmem_limit_bytes=min(int(vmem_bytes), V7X_VMEM_BYTES - 8 * 1024 * 1024))


def _ln(x):
    mu = jnp.mean(x, axis=-1, keepdims=True)
    xc = x - mu
    var = jnp.mean(xc * xc, axis=-1, keepdims=True)
    return xc * lax.rsqrt(var + EPS)


def _dot(a, b):
    return jnp.dot(a, b, preferred_element_type=F32)


def _dot_nt(a, b):
    return lax.dot_general(a, b, (((1,), (1,)), ((), ())), preferred_element_type=F32)


def _dot_tn(a, b):
    return lax.dot_general(a, b, (((0,), (0,)), ((), ())), preferred_element_type=F32)


def _mod_kernel(c_ref, w_ref, b_ref, o_ref):
    c = c_ref[...]
    s = (c * jax.nn.sigmoid(c)).astype(BF16)
    o_ref[0] = _dot(s, w_ref[0].astype(BF16)) + b_ref[0]


def _modulation(c_rows, w_mod, b_mod):
    depth, d, n = w_mod.shape
    rows = c_rows.shape[0]
    tn = 1536
    return pl.pallas_call(
        _mod_kernel,
        grid=(depth, n // tn),
        in_specs=[pl.BlockSpec((rows, d), lambda i, j: (0, 0)),
                  pl.BlockSpec((1, d, tn), lambda i, j: (i, 0, j)),
                  pl.BlockSpec((1, 1, tn), lambda i, j: (i, 0, j))],
        out_specs=pl.BlockSpec((1, rows, tn), lambda i, j: (i, 0, j)),
        out_shape=jax.ShapeDtypeStruct((depth, rows, n), F32),
        compiler_params=_params(("parallel", "parallel"), 4 * d * tn * 4),
        name="modulation",
    )(c_rows, w_mod, b_mod.reshape(depth, 1, n))


def _rope(x, cos, sin_signed):
    w = x.shape[-1]
    lane = lax.broadcasted_iota(jnp.int32, x.shape, 1)
    first = (lane % 32) < 16
    partner = jnp.where(first, pltpu.roll(x, w - 16, 1), pltpu.roll(x, 16, 1))
    return x * cos + partner * sin_signed


def _group_rms(x, bd, g):
    w = bd.shape[0]
    parts = []
    for s in range(0, x.shape[-1], w):
        xs = x[:, s:s + w]
        ms = _dot((xs * xs).astype(BF16), bd) * (1.0 / HEAD_DIM)
        parts.append(xs * lax.rsqrt(ms + EPS))
    y = parts[0] if len(parts) == 1 else jnp.concatenate(parts, axis=-1)
    return y * g


def _ab_in_kernel(use_rope, conv_ch, att_w, kv_w, x_ref, sh_ref, sc_ref, w_ref, b_ref, qg_ref, kg_ref, bd_ref,
                  cos_ref, sin_ref, u_ref, q_ref, k_ref, v_ref):
    h = _ln(x_ref[0]) * (1.0 + sc_ref[0]) + sh_ref[0]
    p = _dot(h.astype(BF16), w_ref[...]) + b_ref[...]
    c1, c2, c3, c4 = conv_ch, 2 * conv_ch, 2 * conv_ch + att_w, 2 * conv_ch + att_w + kv_w
    u_ref[0] = p[:, :c1] * jax.nn.sigmoid(p[:, c1:c2])
    q = _group_rms(p[:, c2:c3], bd_ref[...], qg_ref[...])
    k = _group_rms(p[:, c3:c4], bd_ref[:kv_w, :kv_w], kg_ref[...])
    if use_rope:
        q = _rope(q, cos_ref[...], sin_ref[...])
        k = _rope(k, cos_ref[:, :kv_w], sin_ref[:, :kv_w])
    q_ref[0] = (q * (HEAD_DIM ** -0.5)).astype(BF16)
    k_ref[0] = k.astype(BF16)
    v_ref[0] = p[:, c4:].astype(BF16)


def _ab_in(x, sh, sc, w, b, qg, kg, bd, cos, sin, use_rope, conv_ch, att_w, kv_w):
    bn, l, d = x.shape
    n = w.shape[1]
    tm = min(512, l)
    kern = functools.partial(_ab_in_kernel, use_rope, conv_ch, att_w, kv_w)
    row = lambda i, j: (i, j, 0)
    mod = lambda i, j: (i, 0, 0)
    fix = lambda i, j: (0, 0)
    return pl.pallas_call(
        kern,
        grid=(bn, l // tm),
        in_specs=[pl.BlockSpec((1, tm, d), row), pl.BlockSpec((1, 1, d), mod), pl.BlockSpec((1, 1, d), mod),
                  pl.BlockSpec((d, n), fix), pl.BlockSpec((1, n), fix),
                  pl.BlockSpec((1, att_w), fix), pl.BlockSpec((1, kv_w), fix), pl.BlockSpec(bd.shape, fix),
                  pl.BlockSpec((tm, att_w), lambda i, j: (j, 0)), pl.BlockSpec((tm, att_w), lambda i, j: (j, 0))],
        out_specs=[pl.BlockSpec((1, tm, conv_ch), row), pl.BlockSpec((1, tm, att_w), row),
                   pl.BlockSpec((1, tm, kv_w), row), pl.BlockSpec((1, tm, kv_w), row)],
        out_shape=[jax.ShapeDtypeStruct((bn, l, conv_ch), F32), jax.ShapeDtypeStruct((bn, l, att_w), BF16),
                   jax.ShapeDtypeStruct((bn, l, kv_w), BF16), jax.ShapeDtypeStruct((bn, l, kv_w), BF16)],
        compiler_params=_params(("parallel", "parallel"), 2 * (tm * d * 4 + d * n * 2) + 8 * tm * n * 4),
        name="ab_in_proj",
    )(x, sh, sc, w, b, qg, kg, bd, cos, sin)


def _attn_kernel(q_ref, k_ref, v_ref, o_ref):
    group = ATT_HEADS // ATT_KV_HEADS
    for kh in range(ATT_KV_HEADS):
        kk = k_ref[0, :, kh * HEAD_DIM:(kh + 1) * HEAD_DIM]
        vv = v_ref[0, :, kh * HEAD_DIM:(kh + 1) * HEAD_DIM]
        for g in range(group):
            lo = (kh * group + g) * HEAD_DIM
            s = _dot_nt(q_ref[0, :, lo:lo + HEAD_DIM], kk)
            p = jnp.exp(s - jnp.max(s, axis=-1, keepdims=True))
            denom = jnp.sum(p, axis=-1, keepdims=True)
            o_ref[0, :, lo:lo + HEAD_DIM] = _dot(p.astype(BF16), vv) / denom


def _attention(q, k, v):
    bn, lq, w = q.shape
    s, kvw = k.shape[1], k.shape[2]
    tq = min(256, lq)
    return pl.pallas_call(
        _attn_kernel,
        grid=(bn, lq // tq),
        in_specs=[pl.BlockSpec((1, tq, w), lambda i, j: (i, j, 0)),
                  pl.BlockSpec((1, s, kvw), lambda i, j: (i, 0, 0)),
                  pl.BlockSpec((1, s, kvw), lambda i, j: (i, 0, 0))],
        out_specs=pl.BlockSpec((1, tq, w), lambda i, j: (i, j, 0)),
        out_shape=jax.ShapeDtypeStruct((bn, lq, w), F32),
        compiler_params=_params(("parallel", "parallel"), 8 * tq * s * 4 + 8 * s * kvw * 2),
        name="gqa_attention",
    )(q, k, v)


CONV_ROWS = 32
CONV_HALO = 16


def _conv_kernel(u_ref, w_ref, cb_ref, g_ref, b_ref, o_ref, pad_ref, shift_ref):
    l, ch = u_ref.shape[1], u_ref.shape[2]
    zeros = jnp.zeros((CONV_HALO, ch), F32)
    pad_ref[0:CONV_HALO, :] = zeros
    pad_ref[CONV_HALO + l:2 * CONV_HALO + l, :] = zeros
    pad_ref[CONV_HALO:CONV_HALO + l, :] = u_ref[0]
    first = CONV_HALO - CONV_WIDTH // 2

    def tile(r, carry):
        base = pl.multiple_of(r * CONV_ROWS, CONV_ROWS)
        acc = jnp.zeros((CONV_ROWS, ch), F32)
        win = pad_ref[pl.ds(base, CONV_ROWS + 2 * CONV_HALO), :]
        for res in range(V7X_SUBLANES):
            taps = [j for j in range(CONV_WIDTH) if (first + j) % V7X_SUBLANES == res]
            span = max(first + j - res for j in taps) + CONV_ROWS
            shift_ref[0:span, :] = win[res:res + span, :]
            for j in taps:
                off = first + j - res
                acc = acc + w_ref[j:j + 1, :] * shift_ref[off:off + CONV_ROWS, :]
        y = _ln(acc + cb_ref[...]) * g_ref[...] + b_ref[...]
        o_ref[0, pl.ds(base, CONV_ROWS), :] = y * jax.nn.sigmoid(y)
        return carry

    lax.fori_loop(0, l // CONV_ROWS, tile, 0)


def _conv_group(u, w, cb, g, b):
    bn, l, ch = u.shape
    fix = lambda i: (0, 0)
    return pl.pallas_call(
        _conv_kernel,
        grid=(bn,),
        in_specs=[pl.BlockSpec((1, l, ch), lambda i: (i, 0, 0)), pl.BlockSpec((CONV_WIDTH, ch), fix),
                  pl.BlockSpec((1, ch), fix), pl.BlockSpec((1, ch), fix), pl.BlockSpec((1, ch), fix)],
        out_specs=pl.BlockSpec((1, l, ch), lambda i: (i, 0, 0)),
        out_shape=jax.ShapeDtypeStruct((bn, l, ch), F32),
        scratch_shapes=[pltpu.VMEM((l + 2 * CONV_HALO, ch), F32), pltpu.VMEM((CONV_ROWS + 2 * CONV_HALO, ch), F32)],
        compiler_params=_params(("parallel",), 6 * l * ch * 4),
        name="conv_group",
    )(u, w, cb, g, b)


def _resid_epilogue(y, x_ref, g1_ref, lng_ref, lnb_ref, sh2_ref, sc2_ref, x1_ref, hp_ref):
    x1 = _ln(ALPHA * x_ref[0] + g1_ref[0] * y) * lng_ref[...] + lnb_ref[...]
    x1_ref[0] = x1
    hp_ref[0] = (_ln(x1) * (1.0 + sc2_ref[0]) + sh2_ref[0]).astype(BF16)


def _ab_out_kernel(a_ref, t_ref, w1_ref, w2_ref, b_ref, x_ref, g1_ref, lng_ref, lnb_ref, sh2_ref, sc2_ref,
                   x1_ref, hp_ref):
    y = _dot(a_ref[0].astype(BF16), w1_ref[...]) + _dot(t_ref[0].astype(BF16), w2_ref[...]) + b_ref[...]
    _resid_epilogue(y, x_ref, g1_ref, lng_ref, lnb_ref, sh2_ref, sc2_ref, x1_ref, hp_ref)


def _ab_out(conv, att, w1, w2, b, x, g1, lng, lnb, sh2, sc2):
    bn, l, d = x.shape
    tm = min(512, l)
    row = lambda i, j: (i, j, 0)
    mod = lambda i, j: (i, 0, 0)
    fix = lambda i, j: (0, 0)
    k1, k2 = conv.shape[2], att.shape[2]
    return pl.pallas_call(
        _ab_out_kernel,
        grid=(bn, l // tm),
        in_specs=[pl.BlockSpec((1, tm, k1), row), pl.BlockSpec((1, tm, k2), row),
                  pl.BlockSpec((k1, d), fix), pl.BlockSpec((k2, d), fix), pl.BlockSpec((1, d), fix),
                  pl.BlockSpec((1, tm, d), row), pl.BlockSpec((1, 1, d), mod),
                  pl.BlockSpec((1, d), fix), pl.BlockSpec((1, d), fix),
                  pl.BlockSpec((1, 1, d), mod), pl.BlockSpec((1, 1, d), mod)],
        out_specs=[pl.BlockSpec((1, tm, d), row), pl.BlockSpec((1, tm, d), row)],
        out_shape=[jax.ShapeDtypeStruct((bn, l, d), F32), jax.ShapeDtypeStruct((bn, l, d), BF16)],
        compiler_params=_params(("parallel", "parallel"), 16 * tm * d * 4),
        name="ab_out_proj",
    )(conv, att, w1, w2, b, x, g1, lng, lnb, sh2, sc2)


def _ml_out_kernel(hf_ref, hb_ref, o_ref, ng_ref, w_ref, b_ref, x_ref, g1_ref, lng_ref, lnb_ref, sh2_ref, sc2_ref,
                   x1_ref, hp_ref):
    dv = hf_ref.shape[2] // ML_HEADS
    hs = hf_ref[0] + hb_ref[0]
    parts = [_ln(hs[:, h * dv:(h + 1) * dv]) for h in range(ML_HEADS)]
    hn = jnp.concatenate(parts, axis=-1) * ng_ref[...]
    y = _dot((hn * jax.nn.sigmoid(o_ref[0])).astype(BF16), w_ref[...]) + b_ref[...]
    _resid_epilogue(y, x_ref, g1_ref, lng_ref, lnb_ref, sh2_ref, sc2_ref, x1_ref, hp_ref)


def _ml_out(hf, hb, o, seq_off, ng, w, b, x, g1, lng, lnb, sh2, sc2):
    bn, l, d = x.shape
    vw = o.shape[2]
    tm = math.gcd(256, math.gcd(l, seq_off))
    off = seq_off // tm
    row = lambda i, j: (i, j, 0)
    mod = lambda i, j: (i, 0, 0)
    fix = lambda i, j: (0, 0)
    return pl.pallas_call(
        _ml_out_kernel,
        grid=(bn, l // tm),
        in_specs=[pl.BlockSpec((1, tm, vw), lambda i, j: (i, j + off, 0)),
                  pl.BlockSpec((1, tm, vw), lambda i, j: (i, j + off, 0)),
                  pl.BlockSpec((1, tm, vw), lambda i, j: (i, j + off, 0)),
                  pl.BlockSpec((1, vw), fix), pl.BlockSpec((vw, d), fix), pl.BlockSpec((1, d), fix),
                  pl.BlockSpec((1, tm, d), row), pl.BlockSpec((1, 1, d), mod),
                  pl.BlockSpec((1, d), fix), pl.BlockSpec((1, d), fix),
                  pl.BlockSpec((1, 1, d), mod), pl.BlockSpec((1, 1, d), mod)],
        out_specs=[pl.BlockSpec((1, tm, d), row), pl.BlockSpec((1, tm, d), row)],
        out_shape=[jax.ShapeDtypeStruct((bn, l, d), F32), jax.ShapeDtypeStruct((bn, l, d), BF16)],
        compiler_params=_params(("parallel", "parallel"), 24 * tm * d * 4),
        name="ml_out_proj",
    )(hf, hb, o, ng, w, b, x, g1, lng, lnb, sh2, sc2)


def _ml_in_kernel(qk_w, v_w, x_ref, shl_ref, scl_ref, shc_ref, scc_ref, w_ref, b_ref, wg_ref, bg_ref,
                  q_ref, k_ref, v_ref, o_ref, g_ref):
    is_ctx = pl.program_id(1) == 0
    sh = jnp.where(is_ctx, shc_ref[0], shl_ref[0])
    sc = jnp.where(is_ctx, scc_ref[0], scl_ref[0])
    h = (_ln(x_ref[0]) * (1.0 + sc) + sh).astype(BF16)
    p = _dot(h, w_ref[...]) + b_ref[...]
    dqk = qk_w // ML_HEADS
    q_ref[0] = p[:, :qk_w].astype(BF16)
    k_ref[0] = (p[:, qk_w:2 * qk_w] * (dqk ** -0.5)).astype(BF16)
    v_ref[0] = p[:, 2 * qk_w:2 * qk_w + v_w].astype(BF16)
    o_ref[0] = p[:, 2 * qk_w + v_w:]
    g_ref[0] = (_dot(h, wg_ref[...]) + bg_ref[...])[:, :g_ref.shape[2]]


def _ml_in(xcat, lc, shl, scl, shc, scc, w, b, wg, bg, qk_w, v_w):
    bn, s, d = xcat.shape
    n = w.shape[1]
    ng = 4 * ML_HEADS
    tm = lc
    row = lambda i, j: (i, j, 0)
    mod = lambda i, j: (i, 0, 0)
    one = lambda i, j: (0, 0, 0)
    fix = lambda i, j: (0, 0)
    kern = functools.partial(_ml_in_kernel, qk_w, v_w)
    return pl.pallas_call(
        kern,
        grid=(bn, s // tm),
        in_specs=[pl.BlockSpec((1, tm, d), row), pl.BlockSpec((1, 1, d), mod), pl.BlockSpec((1, 1, d), mod),
                  pl.BlockSpec((1, 1, d), one), pl.BlockSpec((1, 1, d), one),
                  pl.BlockSpec((d, n), fix), pl.BlockSpec((1, n), fix),
                  pl.BlockSpec(wg.shape, fix), pl.BlockSpec(bg.shape, fix)],
        out_specs=[pl.BlockSpec((1, tm, qk_w), row), pl.BlockSpec((1, tm, qk_w), row),
                   pl.BlockSpec((1, tm, v_w), row), pl.BlockSpec((1, tm, v_w), row),
                   pl.BlockSpec((1, tm, ng), row)],
        out_shape=[jax.ShapeDtypeStruct((bn, s, qk_w), BF16), jax.ShapeDtypeStruct((bn, s, qk_w), BF16),
                   jax.ShapeDtypeStruct((bn, s, v_w), BF16), jax.ShapeDtypeStruct((bn, s, v_w), F32),
                   jax.ShapeDtypeStruct((bn, s, ng), F32)],
        compiler_params=_params(("parallel", "parallel"), 2 * (tm * d * 4 + d * n * 2) + 8 * tm * n * 4),
        name="ml_in_proj",
    )(xcat, shl, scl, shc, scc, w, b, wg, bg)


def _log_sigmoid(x):
    return jnp.minimum(x, 0.0) - jnp.log1p(jnp.exp(-jnp.abs(x)))


def _mlstm_kernel(qf_ref, kf_ref, vf_ref, gf_ref, gtf_ref, qb_ref, kb_ref, vb_ref, gb_ref, gtb_ref,
                  hf_ref, hb_ref, ct_ref, n_ref, m_ref):
    @pl.when(pl.program_id(1) == 0)
    def _():
        ct_ref[...] = jnp.zeros_like(ct_ref)
        n_ref[...] = jnp.zeros_like(n_ref)
        m_ref[...] = jnp.zeros_like(m_ref)

    ch = qf_ref.shape[1]
    dk = qf_ref.shape[2] // ML_HEADS
    dv = vf_ref.shape[2] // ML_HEADS
    row = lax.broadcasted_iota(jnp.int32, (ch, ch), 0)
    col = lax.broadcasted_iota(jnp.int32, (ch, ch), 1)
    chains = []
    for backward, q_ref, k_ref, v_ref, g_ref, gt_ref, h_ref in (
            (False, qf_ref, kf_ref, vf_ref, gf_ref, gtf_ref, hf_ref),
            (True, qb_ref, kb_ref, vb_ref, gb_ref, gtb_ref, hb_ref)):
        seen = (col >= row) if backward else (col <= row)
        g = g_ref[0, 0]
        gt = gt_ref[0, 0, 0]
        f_cols = _log_sigmoid(g[:, ML_HEADS:])
        f_rows = _log_sigmoid(gt[ML_HEADS:, :])
        seen_f = seen.astype(F32)
        b_cols = jnp.dot(seen_f, f_cols, precision=lax.Precision.HIGHEST, preferred_element_type=F32)
        b_rows = lax.dot_general(f_rows, seen_f, (((1,), (1,)), ((), ())), precision=lax.Precision.HIGHEST,
                                 preferred_element_type=F32)
        for head in range(ML_HEADS):
            chains.append(dict(
                st=(ML_HEADS if backward else 0) + head, head=head, seen=seen, h_ref=h_ref,
                i_col=g[:, head:head + 1], i_row=gt[head:head + 1, :], f_row=f_rows[head:head + 1, :],
                b_col=b_cols[:, head:head + 1], b_row=b_rows[head:head + 1, :],
                q=q_ref[0, :, head * dk:(head + 1) * dk], k=k_ref[0, :, head * dk:(head + 1) * dk],
                v=v_ref[0, :, head * dv:(head + 1) * dv]))
    for c in chains:
        c["m"] = m_ref[c["st"]]
        c["a_col"] = c["b_col"] + c["m"]
        c["dlog"] = jnp.where(c["seen"], c["b_col"] - c["b_row"] + c["i_row"], NEG_INF)
        c["qk"] = _dot_nt(c["q"], c["k"])
        c["ct"] = ct_ref[c["st"]]
        c["qc"] = _dot(c["q"], c["ct"].astype(BF16))
    for c in chains:
        c["mt"] = jnp.maximum(c["a_col"], jnp.max(c["dlog"], axis=1, keepdims=True))
        c["qn"] = jnp.sum(c["q"].astype(F32) * n_ref[c["st"]], axis=1, keepdims=True)
        c["bl"] = jnp.sum(c["f_row"], axis=1, keepdims=True)
        c["wl"] = c["bl"] - c["b_col"] + c["i_col"]
    for c in chains:
        c["smat"] = c["qk"] * jnp.exp(c["dlog"] - c["mt"])
        c["aw"] = jnp.exp(c["a_col"] - c["mt"])
        c["mn"] = jnp.maximum(c["bl"] + c["m"], jnp.max(c["wl"], axis=0, keepdims=True))
    for c in chains:
        c["sv"] = _dot(c["smat"].astype(BF16), c["v"])
        c["den"] = jnp.sum(c["smat"], axis=1, keepdims=True) + c["aw"] * c["qn"]
        wc = jnp.exp(c["wl"] - c["mn"])
        c["wc"] = wc
        c["kv"] = _dot_tn(c["k"], (wc * c["v"].astype(F32)).astype(BF16))
    for c in chains:
        head, st = c["head"], c["st"]
        num = c["sv"] + c["aw"] * c["qc"]
        c["h_ref"][0, :, head * dv:(head + 1) * dv] = num / jnp.maximum(jnp.abs(c["den"]), jnp.exp(-c["mt"]))
        dc = jnp.exp(c["bl"] + c["m"] - c["mn"])
        ct_ref[st] = dc * c["ct"] + c["kv"]
        n_ref[st] = dc * n_ref[st] + jnp.sum(c["wc"] * c["k"].astype(F32), axis=0, keepdims=True)
        m_ref[st] = c["mn"]


def _mlstm(q, k, v, g2, gt2, n_ctx_chunks):
    bn, s, qk_w = q.shape
    v_w = v.shape[2]
    ch = ML_CHUNK
    nc = s // ch
    ncx = n_ctx_chunks

    def back(t):
        return jnp.where(t < ncx, ncx - 1 - t, nc + ncx - 1 - t)

    def specs(d, chunk):
        seq = lambda b, t: (b, chunk(t), 0)
        return [pl.BlockSpec((1, ch, qk_w), seq), pl.BlockSpec((1, ch, qk_w), seq), pl.BlockSpec((1, ch, v_w), seq),
                pl.BlockSpec((1, 1, ch, 2 * ML_HEADS), lambda b, t: (d, b, chunk(t), 0)),
                pl.BlockSpec((1, 1, 1, 2 * ML_HEADS, ch), lambda b, t: (d, b, chunk(t), 0, 0))]

    fwd = lambda t: t
    h_shape = jax.ShapeDtypeStruct((bn, s, v_w), F32)
    states = 2 * ML_HEADS
    return pl.pallas_call(
        _mlstm_kernel,
        grid=(bn, nc),
        in_specs=specs(0, fwd) + specs(1, back),
        out_specs=[pl.BlockSpec((1, ch, v_w), lambda b, t: (b, t, 0)),
                   pl.BlockSpec((1, ch, v_w), lambda b, t: (b, back(t), 0))],
        out_shape=[h_shape, h_shape],
        scratch_shapes=[pltpu.VMEM((states, qk_w // ML_HEADS, v_w // ML_HEADS), F32),
                        pltpu.VMEM((states, 1, qk_w // ML_HEADS), F32),
                        pltpu.VMEM((states, 1, 1), F32)],
        compiler_params=_params(("parallel", "arbitrary"), 16 * 1024 * 1024),
        name="mlstm_scan",
    )(q, k, v, g2, gt2, q, k, v, g2, gt2)


def _top_keys(s, vals_ref):
    rank = jnp.full(s.shape, float(PEER_TOPK), F32)
    for i in range(PEER_TOPK):
        m = jnp.max(s, axis=0, keepdims=True)
        hit = s == m
        rank = jnp.where(hit, float(i), rank)
        s = jnp.where(hit, NEG_INF, s)
        vals_ref[i:i + 1, :] = m
    return rank


def _peer_sel_kernel(hp_ref, wqt_ref, k1_ref, k2_ref, r2_ref, e2_ref, kk_ref, cc_ref, v1_ref, v2_ref, cnt_ref):
    half = N_KEYS
    qt = _dot_nt(wqt_ref[...], hp_ref[...])
    for h in range(PEER_HEADS):
        q1 = qt[(2 * h) * half:(2 * h + 1) * half].astype(BF16)
        q2 = qt[(2 * h + 1) * half:(2 * h + 2) * half].astype(BF16)
        s1 = _dot(k1_ref[h], q1)
        s2 = _dot(k2_ref[h], q2)
        r1 = _top_keys(s1, v1_ref)
        r2 = _top_keys(s2, v2_ref)
        v1 = v1_ref[...]
        v2 = v2_ref[...]
        cand = jnp.concatenate(
            [v1 + v2[0:1]] + [v1[0:8] + v2[j:j + 1] for j in range(1, 8)] + [v1[0:1] + v2[8:16]], axis=0)
        work = cand
        theta = None
        for _ in range(PEER_TOPK):
            theta = jnp.max(work, axis=0, keepdims=True)
            work = jnp.where(work == theta, NEG_INF, work)
        smax = v1[0:1] + v2[0:1]
        sel = cand >= theta
        z = jnp.sum(jnp.where(sel, jnp.exp(cand - smax), 0.0), axis=0, keepdims=True)
        picked = sel.astype(F32)
        cnt_ref[...] = picked[0:16]
        cnt_ref[0:8, :] += functools.reduce(jnp.add, [picked[8 + 8 * j:16 + 8 * j] for j in range(1, 8)])
        cnt_ref[0:1, :] += jnp.sum(picked[72:80], axis=0, keepdims=True)
        kk = jnp.zeros(s1.shape, F32)
        for i in range(PEER_TOPK):
            kk = jnp.where(r1 == float(i), cnt_ref[i:i + 1, :], kk)
        r2 = r2.astype(BF16)
        e2 = jnp.exp(s2 - v2[0:1]).astype(BF16)
        for grp in range(N_KEYS // V7X_BF16_ROWS):
            r2_ref[0, h, grp] = r2[grp * V7X_BF16_ROWS:(grp + 1) * V7X_BF16_ROWS]
            e2_ref[0, h, grp] = e2[grp * V7X_BF16_ROWS:(grp + 1) * V7X_BF16_ROWS]
        kk_ref[0, h] = kk.astype(BF16)
        cc_ref[0, h] = (jnp.exp(s1 - v1[0:1]) / z).astype(BF16)


def _peer_select(hp, wqt, k1, k2, t):
    nt, d = hp.shape
    nb = nt // t
    sel_spec = pl.BlockSpec((1, PEER_HEADS, N_KEYS, t), lambda i: (i, 0, 0, 0))
    sel_shape = jax.ShapeDtypeStruct((nb, PEER_HEADS, N_KEYS, t), BF16)
    groups = N_KEYS // V7X_BF16_ROWS
    pk_spec = pl.BlockSpec((1, PEER_HEADS, groups, V7X_BF16_ROWS, t), lambda i: (i, 0, 0, 0, 0))
    pk_shape = jax.ShapeDtypeStruct((nb, PEER_HEADS, groups, V7X_BF16_ROWS, t), BF16)
    return pl.pallas_call(
        _peer_sel_kernel,
        grid=(nb,),
        in_specs=[pl.BlockSpec((t, d), lambda i: (i, 0)), pl.BlockSpec(wqt.shape, lambda i: (0, 0)),
                  pl.BlockSpec(k1.shape, lambda i: (0, 0, 0)), pl.BlockSpec(k2.shape, lambda i: (0, 0, 0))],
        out_specs=[pk_spec, pk_spec, sel_spec, sel_spec],
        out_shape=[pk_shape, pk_shape, sel_shape, sel_shape],
        scratch_shapes=[pltpu.VMEM((PEER_TOPK, t), F32), pltpu.VMEM((PEER_TOPK, t), F32),
                        pltpu.VMEM((PEER_TOPK, t), F32)],
        compiler_params=_params(("parallel",), 40 * 1024 * 1024),
        name="peer_select",
    )(hp, wqt, k1, k2)


def _gelu(x):
    return 0.5 * x * (1.0 + lax.erf(x * (2.0 ** -0.5)))


def _peer_dense_kernel(hp_ref, u_ref, vt_ref, r2_ref, e2_ref, kk_ref, cc_ref, x1_ref, g2_ref, lng_ref, lnb_ref,
                       x2_ref, acc_ref, w_ref, r2s_ref, e2s_ref, act_ref):
    j = pl.program_id(1)
    tn, t = u_ref.shape[0], hp_ref.shape[0]
    na = tn // N_KEYS

    @pl.when(j == 0)
    def _():
        acc_ref[...] = jnp.zeros_like(acc_ref)
        r2s_ref[...] = r2_ref[0]
        e2s_ref[...] = e2_ref[0]

    a_rows = pl.ds(pl.multiple_of(j * na, na), na)
    tile = (N_KEYS // V7X_BF16_ROWS, V7X_BF16_ROWS, V7X_LANES)
    for ag in range(0, na, PEER_A_GROUP):
        rows = slice(ag * N_KEYS, (ag + PEER_A_GROUP) * N_KEYS)
        act_ref[...] = _gelu(_dot_nt(u_ref[rows, :], hp_ref[...])).astype(BF16).reshape(act_ref.shape)

        for lt in range(t // V7X_LANES):
            lanes = slice(lt * V7X_LANES, (lt + 1) * V7X_LANES)
            gsum = [jnp.zeros(tile, BF16) for _ in range(PEER_A_GROUP)]
            for h in range(PEER_HEADS):
                r2 = r2s_ref[h, :, :, lanes]
                e2 = e2s_ref[h, :, :, lanes]
                kk = kk_ref[0, h, a_rows, lanes].astype(F32)
                cc = cc_ref[0, h, a_rows, lanes].astype(F32)
                for i in range(PEER_A_GROUP):
                    al = ag + i
                    kk_t = jnp.broadcast_to(kk[al:al + 1], tile[1:]).astype(BF16)[None]
                    cc_t = jnp.broadcast_to(cc[al:al + 1], tile[1:]).astype(BF16)[None]
                    gsum[i] = jnp.where(r2 < kk_t, gsum[i] + e2 * cc_t, gsum[i])
            for i in range(PEER_A_GROUP):
                w_ref[(ag + i) * tile[0]:(ag + i + 1) * tile[0], :, lanes] = (
                    gsum[i] * act_ref[i * tile[0]:(i + 1) * tile[0], :, lanes])
    acc_ref[...] += _dot(vt_ref[...], w_ref[...].reshape(tn, t))

    @pl.when(j == pl.num_programs(1) - 1)
    def _():
        y = ALPHA * x1_ref[...] + g2_ref[0] * acc_ref[...].T
        x2_ref[...] = _ln(y) * lng_ref[...] + lnb_ref[...]


def _peer_dense(hp, u, vt, layer, sel, x1, g2, blocks_per_row, lng, lnb, t, tn):
    nt, d = hp.shape
    ne = u.shape[1]
    sel_spec = pl.BlockSpec((1, PEER_HEADS, N_KEYS, t), lambda i, j: (i, 0, 0, 0))
    pk_spec = pl.BlockSpec((1, PEER_HEADS, N_KEYS // V7X_BF16_ROWS, V7X_BF16_ROWS, t), lambda i, j: (i, 0, 0, 0, 0))
    fix = lambda i, j: (0, 0)
    return pl.pallas_call(
        _peer_dense_kernel,
        grid=(nt // t, ne // tn),
        in_specs=[pl.BlockSpec((t, d), lambda i, j: (i, 0)), pl.BlockSpec((None, tn, d), lambda i, j: (layer, j, 0)),
                  pl.BlockSpec((None, d, tn), lambda i, j: (layer, 0, j)), pk_spec, pk_spec, sel_spec, sel_spec,
                  pl.BlockSpec((t, d), lambda i, j: (i, 0)),
                  pl.BlockSpec((1, 1, d), lambda i, j: (i // blocks_per_row, 0, 0)),
                  pl.BlockSpec((1, d), fix), pl.BlockSpec((1, d), fix)],
        out_specs=pl.BlockSpec((t, d), lambda i, j: (i, 0)),
        out_shape=jax.ShapeDtypeStruct((nt, d), F32),
        scratch_shapes=[pltpu.VMEM((d, t), F32), pltpu.VMEM((tn // V7X_BF16_ROWS, V7X_BF16_ROWS, t), BF16),
                        pltpu.VMEM(pk_spec.block_shape[1:], BF16), pltpu.VMEM(pk_spec.block_shape[1:], BF16),
                        pltpu.VMEM((PEER_A_GROUP * N_KEYS // V7X_BF16_ROWS, V7X_BF16_ROWS, t), BF16)],
        compiler_params=_params(("parallel", "arbitrary"), 48 * 1024 * 1024),
        name="peer_dense",
    )(hp, u, vt, *sel, x1, g2, lng, lnb)


def _peer_block(x1, hp, g2, wqt, k1, k2, u, vt, layer, lng, lnb):
    bn, l, d = x1.shape
    t = min(512, l)
    tn = 16 * N_KEYS
    hp2 = hp.reshape(bn * l, d)
    sel = _peer_select(hp2, wqt, k1, k2, t)
    x2 = _peer_dense(hp2, u, vt, layer, sel, x1.reshape(bn * l, d), g2, l // t, lng, lnb, t, tn)
    return x2.reshape(bn, l, d)


def _rope_tables(l, att_w):
    rows = l // GRID_W
    row = jnp.repeat(jnp.arange(rows, dtype=F32), GRID_W)
    col = jnp.tile(jnp.arange(GRID_W, dtype=F32), rows)
    f = HEAD_DIM // 4
    inv_freq = ROPE_BASE ** (-jnp.arange(f, dtype=F32) / f)
    ar = row[:, None] * inv_freq[None, :]
    ac = col[:, None] * inv_freq[None, :]
    cos = jnp.concatenate([jnp.cos(ar), jnp.cos(ar), jnp.cos(ac), jnp.cos(ac)], axis=-1)
    sin = jnp.concatenate([-jnp.sin(ar), jnp.sin(ar), -jnp.sin(ac), jnp.sin(ac)], axis=-1)
    reps = att_w // HEAD_DIM
    return jnp.tile(cos, (1, reps)), jnp.tile(sin, (1, reps))


def _block_diag_ones(width, group):
    idx = jnp.arange(width) // group
    return (idx[:, None] == idx[None, :]).astype(BF16)


def kernel(x, c, ctx, c_ctx, w_mod, b_mod, ln1_g, ln1_b, ln2_g, ln2_b, ab_w_in, ab_b_in, ab_conv_w, ab_conv_b,
           ab_conv_ln_g, ab_conv_ln_b, ab_q_norm_g, ab_k_norm_g, ab_w_out, ab_b_out, ml_w_in, ml_b_in, ml_norm_g,
           ml_w_out, ml_b_out, peer_w_q, peer_k1, peer_k2, peer_u, peer_v):
    bsz, l, d = x.shape
    lc = ctx.shape[1]
    assert l % GRID_W == 0 and lc % ML_CHUNK == 0 and l % lc == 0

    rows = -(-(bsz + 1) // 8) * 8
    c_rows = jnp.zeros((rows, d), F32).at[:bsz].set(c).at[bsz].set(c_ctx)
    mod = _modulation(c_rows, w_mod, b_mod)

    def mod_rows(i):
        m = mod[i].reshape(rows, N_MOD, d)
        lat = [m[:bsz, k][:, None, :] for k in range(N_MOD)]
        cx = [m[bsz:bsz + 1, k][:, None, :] for k in range(N_MOD)]
        return lat, cx

    row2 = lambda v: v.reshape(1, -1)

    conv_ch = ab_conv_w.shape[2]
    kv_w = ATT_KV_HEADS * HEAD_DIM
    att_w = ATT_HEADS * HEAD_DIM
    (sh1l, sc1l, g1l, sh2l, sc2l, g2l), (sh1c, sc1c, g1c, sh2c, sc2c, g2c) = mod_rows(0)
    bcast = lambda v: jnp.broadcast_to(v, (bsz, 1, d))
    w_in = ab_w_in[0].astype(BF16)
    b_in = row2(ab_b_in[0])
    qg = row2(jnp.tile(ab_q_norm_g[0], ATT_HEADS))
    kg = row2(jnp.tile(ab_k_norm_g[0], ATT_KV_HEADS))
    bd = _block_diag_ones(2 * V7X_LANES, HEAD_DIM)
    cos, sin = _rope_tables(l, att_w)
    ul, ql, kl, vl = _ab_in(x, sh1l, sc1l, w_in, b_in, qg, kg, bd, cos, sin, True, conv_ch, att_w, kv_w)
    uc, qc, kc, vc = _ab_in(ctx, bcast(sh1c), bcast(sc1c), w_in, b_in, qg, kg, bd, cos[:lc], sin[:lc], False,
                            conv_ch, att_w, kv_w)
    conv_args = (ab_conv_w[0], row2(ab_conv_b[0]), row2(ab_conv_ln_g[0]), row2(ab_conv_ln_b[0]))
    conv_l = _conv_group(ul, *conv_args)
    conv_c = _conv_group(uc, *conv_args)
    att_l = _attention(ql, jnp.concatenate([kc, kl], axis=1), jnp.concatenate([vc, vl], axis=1))
    att_c = _attention(qc, kc, vc)
    w_out = ab_w_out[0].astype(BF16)
    out_args = (w_out[:conv_ch], w_out[conv_ch:], row2(ab_b_out[0]))
    ln1 = (row2(ln1_g[0]), row2(ln1_b[0]))
    x1, hpl = _ab_out(conv_l, att_l, *out_args, x, g1l, *ln1, sh2l, sc2l)
    c1, hpc = _ab_out(conv_c, att_c, *out_args, ctx, bcast(g1c), *ln1, bcast(sh2c), bcast(sc2c))

    assert peer_k1.shape[3] == N_KEYS
    u_all = peer_u.astype(BF16)
    vt_all = jnp.swapaxes(peer_v, 1, 2).astype(BF16)

    def peer_weights(i):
        wqt = peer_w_q[i].T.astype(BF16)
        return (wqt, peer_k1[i].astype(BF16), peer_k2[i].astype(BF16), u_all, vt_all, i,
                row2(ln2_g[i]), row2(ln2_b[i]))

    pw = peer_weights(0)
    x = _peer_block(x1, hpl, g2l, *pw)
    ctx = _peer_block(c1.reshape(1, bsz * lc, d), hpc.reshape(1, bsz * lc, d), g2c, *pw).reshape(bsz, lc, d)

    (sh1l, sc1l, g1l, sh2l, sc2l, g2l), (sh1c, sc1c, _, _, _, _) = mod_rows(1)
    qk_w = ML_HEADS * (d // 8)
    v_w = ML_HEADS * (d // 4)
    n_main = 2 * qk_w + 2 * v_w
    w_in = ml_w_in[0]
    wg = jnp.zeros((d, V7X_LANES), F32).at[:, :4 * ML_HEADS].set(w_in[:, n_main:]).astype(BF16)
    bg = jnp.zeros((1, V7X_LANES), F32).at[:, :4 * ML_HEADS].set(ml_b_in[0][n_main:])
    xcat = jnp.concatenate([ctx, x], axis=1)
    q, k, v, o, g = _ml_in(xcat, lc, sh1l, sc1l, sh1c, sc1c, w_in[:, :n_main].astype(BF16),
                           row2(ml_b_in[0][:n_main]), wg, bg, qk_w, v_w)
    s = lc + l
    g2 = g.reshape(bsz, s, 2, 2 * ML_HEADS).transpose(2, 0, 1, 3)
    gt2 = g2.reshape(2, bsz, s // ML_CHUNK, ML_CHUNK, 2 * ML_HEADS).transpose(0, 1, 2, 4, 3)
    hf, hb = _mlstm(q, k, v, g2, gt2, lc // ML_CHUNK)
    x1, hpl = _ml_out(hf, hb, o, lc, row2(ml_norm_g[0]), ml_w_out[0].astype(BF16), row2(ml_b_out[0]), x, g1l,
                      row2(ln1_g[1]), row2(ln1_b[1]), sh2l, sc2l)
    return _peer_block(x1, hpl, g2l, *peer_weights(1))
```

```python
import functools
import math

import jax
import jax.numpy as jnp
from jax import lax
from jax.experimental import pallas as pl
from jax.experimental.pallas import tpu as pltpu

F32 = jnp.float32
BF16 = jnp.bfloat16

DEPTH = 2
EPS = 1e-6
ALPHA = (2 * DEPTH) ** 0.25
N_MOD = 6
GRID_W = 64

CONV_WIDTH = 31
ATT_HEADS = 8
ATT_KV_HEADS = 2
HEAD_DIM = 64
ROPE_BASE = 10000.0

ML_HEADS = 4
ML_CHUNK = 64

PEER_HEADS = 8
N_KEYS = 128
PEER_TOPK = 16
PEER_A_GROUP = 4

V7X_VMEM_BYTES = 64 * 1024 * 1024
V7X_LANES = 128
V7X_SUBLANES = 8
V7X_BF16_ROWS = 16
NEG_INF = float("-inf")


def _params(semantics, vmem_bytes):
    return pltpu.CompilerParams(dimension_semantics=semantics,
                                vmem_limit_bytes=min(int(vmem_bytes), V7X_VMEM_BYTES - 8 * 1024 * 1024))


def _ln(x):
    mu = jnp.mean(x, axis=-1, keepdims=True)
    xc = x - mu
    var = jnp.mean(xc * xc, axis=-1, keepdims=True)
    return xc * lax.rsqrt(var + EPS)


def _dot(a, b):
    return jnp.dot(a, b, preferred_element_type=F32)


def _dot_nt(a, b):
    return lax.dot_general(a, b, (((1,), (1,)), ((), ())), preferred_element_type=F32)


def _dot_tn(a, b):
    return lax.dot_general(a, b, (((0,), (0,)), ((), ())), preferred_element_type=F32)


def _mod_kernel(c_ref, w_ref, b_ref, o_ref):
    c = c_ref[...]
    s = (c * jax.nn.sigmoid(c)).astype(BF16)
    o_ref[0] = _dot(s, w_ref[0].astype(BF16)) + b_ref[0]


def _modulation(c_rows, w_mod, b_mod):
    depth, d, n = w_mod.shape
    rows = c_rows.shape[0]
    tn = 1536
    return pl.pallas_call(
        _mod_kernel,
        grid=(depth, n // tn),
        in_specs=[pl.BlockSpec((rows, d), lambda i, j: (0, 0)),
                  pl.BlockSpec((1, d, tn), lambda i, j: (i, 0, j)),
                  pl.BlockSpec((1, 1, tn), lambda i, j: (i, 0, j))],
        out_specs=pl.BlockSpec((1, rows, tn), lambda i, j: (i, 0, j)),
        out_shape=jax.ShapeDtypeStruct((depth, rows, n), F32),
        compiler_params=_params(("parallel", "parallel"), 4 * d * tn * 4),
        name="modulation",
    )(c_rows, w_mod, b_mod.reshape(depth, 1, n))


def _rope(x, cos, sin_signed):
    w = x.shape[-1]
    lane = lax.broadcasted_iota(jnp.int32, x.shape, 1)
    first = (lane % 32) < 16
    partner = jnp.where(first, pltpu.roll(x, w - 16, 1), pltpu.roll(x, 16, 1))
    return x * cos + partner * sin_signed


def _group_rms(x, bd, g):
    w = bd.shape[0]
    parts = []
    for s in range(0, x.shape[-1], w):
        xs = x[:, s:s + w]
        ms = _dot((xs * xs).astype(BF16), bd) * (1.0 / HEAD_DIM)
        parts.append(xs * lax.rsqrt(ms + EPS))
    y = parts[0] if len(parts) == 1 else jnp.concatenate(parts, axis=-1)
    return y * g


def _ab_in_kernel(use_rope, conv_ch, att_w, kv_w, x_ref, sh_ref, sc_ref, w_ref, b_ref, qg_ref, kg_ref, bd_ref,
                  cos_ref, sin_ref, u_ref, q_ref, k_ref, v_ref):
    h = _ln(x_ref[0]) * (1.0 + sc_ref[0]) + sh_ref[0]
    p = _dot(h.astype(BF16), w_ref[...]) + b_ref[...]
    c1, c2, c3, c4 = conv_ch, 2 * conv_ch, 2 * conv_ch + att_w, 2 * conv_ch + att_w + kv_w
    u_ref[0] = p[:, :c1] * jax.nn.sigmoid(p[:, c1:c2])
    q = _group_rms(p[:, c2:c3], bd_ref[...], qg_ref[...])
    k = _group_rms(p[:, c3:c4], bd_ref[:kv_w, :kv_w], kg_ref[...])
    if use_rope:
        q = _rope(q, cos_ref[...], sin_ref[...])
        k = _rope(k, cos_ref[:, :kv_w], sin_ref[:, :kv_w])
    q_ref[0] = (q * (HEAD_DIM ** -0.5)).astype(BF16)
    k_ref[0] = k.astype(BF16)
    v_ref[0] = p[:, c4:].astype(BF16)


def _ab_in(x, sh, sc, w, b, qg, kg, bd, cos, sin, use_rope, conv_ch, att_w, kv_w):
    bn, l, d = x.shape
    n = w.shape[1]
    tm = min(512, l)
    kern = functools.partial(_ab_in_kernel, use_rope, conv_ch, att_w, kv_w)
    row = lambda i, j: (i, j, 0)
    mod = lambda i, j: (i, 0, 0)
    fix = lambda i, j: (0, 0)
    return pl.pallas_call(
        kern,
        grid=(bn, l // tm),
        in_specs=[pl.BlockSpec((1, tm, d), row), pl.BlockSpec((1, 1, d), mod), pl.BlockSpec((1, 1, d), mod),
                  pl.BlockSpec((d, n), fix), pl.BlockSpec((1, n), fix),
                  pl.BlockSpec((1, att_w), fix), pl.BlockSpec((1, kv_w), fix), pl.BlockSpec(bd.shape, fix),
                  pl.BlockSpec((tm, att_w), lambda i, j: (j, 0)), pl.BlockSpec((tm, att_w), lambda i, j: (j, 0))],
        out_specs=[pl.BlockSpec((1, tm, conv_ch), row), pl.BlockSpec((1, tm, att_w), row),
                   pl.BlockSpec((1, tm, kv_w), row), pl.BlockSpec((1, tm, kv_w), row)],
        out_shape=[jax.ShapeDtypeStruct((bn, l, conv_ch), F32), jax.ShapeDtypeStruct((bn, l, att_w), BF16),
                   jax.ShapeDtypeStruct((bn, l, kv_w), BF16), jax.ShapeDtypeStruct((bn, l, kv_w), BF16)],
        compiler_params=_params(("parallel", "parallel"), 2 * (tm * d * 4 + d * n * 2) + 8 * tm * n * 4),
        name="ab_in_proj",
    )(x, sh, sc, w, b, qg, kg, bd, cos, sin)


def _attn_kernel(q_ref, k_ref, v_ref, o_ref):
    group = ATT_HEADS // ATT_KV_HEADS
    for kh in range(ATT_KV_HEADS):
        kk = k_ref[0, :, kh * HEAD_DIM:(kh + 1) * HEAD_DIM]
        vv = v_ref[0, :, kh * HEAD_DIM:(kh + 1) * HEAD_DIM]
        for g in range(group):
            lo = (kh * group + g) * HEAD_DIM
            s = _dot_nt(q_ref[0, :, lo:lo + HEAD_DIM], kk)
            p = jnp.exp(s - jnp.max(s, axis=-1, keepdims=True))
            denom = jnp.sum(p, axis=-1, keepdims=True)
            o_ref[0, :, lo:lo + HEAD_DIM] = _dot(p.astype(BF16), vv) / denom


def _attention(q, k, v):
    bn, lq, w = q.shape
    s, kvw = k.shape[1], k.shape[2]
    tq = min(256, lq)
    return pl.pallas_call(
        _attn_kernel,
        grid=(bn, lq // tq),
        in_specs=[pl.BlockSpec((1, tq, w), lambda i, j: (i, j, 0)),
                  pl.BlockSpec((1, s, kvw), lambda i, j: (i, 0, 0)),
                  pl.BlockSpec((1, s, kvw), lambda i, j: (i, 0, 0))],
        out_specs=pl.BlockSpec((1, tq, w), lambda i, j: (i, j, 0)),
        out_shape=jax.ShapeDtypeStruct((bn, lq, w), F32),
        compiler_params=_params(("parallel", "parallel"), 8 * tq * s * 4 + 8 * s * kvw * 2),
        name="gqa_attention",
    )(q, k, v)


CONV_ROWS = 32
CONV_HALO = 16


def _conv_kernel(u_ref, w_ref, cb_ref, g_ref, b_ref, o_ref, pad_ref, shift_ref):
    l, ch = u_ref.shape[1], u_ref.shape[2]
    zeros = jnp.zeros((CONV_HALO, ch), F32)
    pad_ref[0:CONV_HALO, :] = zeros
    pad_ref[CONV_HALO + l:2 * CONV_HALO + l, :] = zeros
    pad_ref[CONV_HALO:CONV_HALO + l, :] = u_ref[0]
    first = CONV_HALO - CONV_WIDTH // 2

    def tile(r, carry):
        base = pl.multiple_of(r * CONV_ROWS, CONV_ROWS)
        acc = jnp.zeros((CONV_ROWS, ch), F32)
        win = pad_ref[pl.ds(base, CONV_ROWS + 2 * CONV_HALO), :]
        for res in range(V7X_SUBLANES):
            taps = [j for j in range(CONV_WIDTH) if (first + j) % V7X_SUBLANES == res]
            span = max(first + j - res for j in taps) + CONV_ROWS
            shift_ref[0:span, :] = win[res:res + span, :]
            for j in taps:
                off = first + j - res
                acc = acc + w_ref[j:j + 1, :] * shift_ref[off:off + CONV_ROWS, :]
        y = _ln(acc + cb_ref[...]) * g_ref[...] + b_ref[...]
        o_ref[0, pl.ds(base, CONV_ROWS), :] = y * jax.nn.sigmoid(y)
        return carry

    lax.fori_loop(0, l // CONV_ROWS, tile, 0)


def _conv_group(u, w, cb, g, b):
    bn, l, ch = u.shape
    fix = lambda i: (0, 0)
    return pl.pallas_call(
        _conv_kernel,
        grid=(bn,),
        in_specs=[pl.BlockSpec((1, l, ch), lambda i: (i, 0, 0)), pl.BlockSpec((CONV_WIDTH, ch), fix),
                  pl.BlockSpec((1, ch), fix), pl.BlockSpec((1, ch), fix), pl.BlockSpec((1, ch), fix)],
        out_specs=pl.BlockSpec((1, l, ch), lambda i: (i, 0, 0)),
        out_shape=jax.ShapeDtypeStruct((bn, l, ch), F32),
        scratch_shapes=[pltpu.VMEM((l + 2 * CONV_HALO, ch), F32), pltpu.VMEM((CONV_ROWS + 2 * CONV_HALO, ch), F32)],
        compiler_params=_params(("parallel",), 6 * l * ch * 4),
        name="conv_group",
    )(u, w, cb, g, b)


def _resid_epilogue(y, x_ref, g1_ref, lng_ref, lnb_ref, sh2_ref, sc2_ref, x1_ref, hp_ref):
    x1 = _ln(ALPHA * x_ref[0] + g1_ref[0] * y) * lng_ref[...] + lnb_ref[...]
    x1_ref[0] = x1
    hp_ref[0] = (_ln(x1) * (1.0 + sc2_ref[0]) + sh2_ref[0]).astype(BF16)


def _ab_out_kernel(a_ref, t_ref, w1_ref, w2_ref, b_ref, x_ref, g1_ref, lng_ref, lnb_ref, sh2_ref, sc2_ref,
                   x1_ref, hp_ref):
    y = _dot(a_ref[0].astype(BF16), w1_ref[...]) + _dot(t_ref[0].astype(BF16), w2_ref[...]) + b_ref[...]
    _resid_epilogue(y, x_ref, g1_ref, lng_ref, lnb_ref, sh2_ref, sc2_ref, x1_ref, hp_ref)


def _ab_out(conv, att, w1, w2, b, x, g1, lng, lnb, sh2, sc2):
    bn, l, d = x.shape
    tm = min(512, l)
    row = lambda i, j: (i, j, 0)
    mod = lambda i, j: (i, 0, 0)
    fix = lambda i, j: (0, 0)
    k1, k2 = conv.shape[2], att.shape[2]
    return pl.pallas_call(
        _ab_out_kernel,
        grid=(bn, l // tm),
        in_specs=[pl.BlockSpec((1, tm, k1), row), pl.BlockSpec((1, tm, k2), row),
                  pl.BlockSpec((k1, d), fix), pl.BlockSpec((k2, d), fix), pl.BlockSpec((1, d), fix),
                  pl.BlockSpec((1, tm, d), row), pl.BlockSpec((1, 1, d), mod),
                  pl.BlockSpec((1, d), fix), pl.BlockSpec((1, d), fix),
                  pl.BlockSpec((1, 1, d), mod), pl.BlockSpec((1, 1, d), mod)],
        out_specs=[pl.BlockSpec((1, tm, d), row), pl.BlockSpec((1, tm, d), row)],
        out_shape=[jax.ShapeDtypeStruct((bn, l, d), F32), jax.ShapeDtypeStruct((bn, l, d), BF16)],
        compiler_params=_params(("parallel", "parallel"), 16 * tm * d * 4),
        name="ab_out_proj",
    )(conv, att, w1, w2, b, x, g1, lng, lnb, sh2, sc2)


def _ml_out_kernel(hf_ref, hb_ref, o_ref, ng_ref, w_ref, b_ref, x_ref, g1_ref, lng_ref, lnb_ref, sh2_ref, sc2_ref,
                   x1_ref, hp_ref):
    dv = hf_ref.shape[2] // ML_HEADS
    hs = hf_ref[0] + hb_ref[0]
    parts = [_ln(hs[:, h * dv:(h + 1) * dv]) for h in range(ML_HEADS)]
    hn = jnp.concatenate(parts, axis=-1) * ng_ref[...]
    y = _dot((hn * jax.nn.sigmoid(o_ref[0])).astype(BF16), w_ref[...]) + b_ref[...]
    _resid_epilogue(y, x_ref, g1_ref, lng_ref, lnb_ref, sh2_ref, sc2_ref, x1_ref, hp_ref)


def _ml_out(hf, hb, o, seq_off, ng, w, b, x, g1, lng, lnb, sh2, sc2):
    bn, l, d = x.shape
    vw = o.shape[2]
    tm = math.gcd(256, math.gcd(l, seq_off))
    off = seq_off // tm
    row = lambda i, j: (i, j, 0)
    mod = lambda i, j: (i, 0, 0)
    fix = lambda i, j: (0, 0)
    return pl.pallas_call(
        _ml_out_kernel,
        grid=(bn, l // tm),
        in_specs=[pl.BlockSpec((1, tm, vw), lambda i, j: (i, j + off, 0)),
                  pl.BlockSpec((1, tm, vw), lambda i, j: (i, j + off, 0)),
                  pl.BlockSpec((1, tm, vw), lambda i, j: (i, j + off, 0)),
                  pl.BlockSpec((1, vw), fix), pl.BlockSpec((vw, d), fix), pl.BlockSpec((1, d), fix),
                  pl.BlockSpec((1, tm, d), row), pl.BlockSpec((1, 1, d), mod),
                  pl.BlockSpec((1, d), fix), pl.BlockSpec((1, d), fix),
                  pl.BlockSpec((1, 1, d), mod), pl.BlockSpec((1, 1, d), mod)],
        out_specs=[pl.BlockSpec((1, tm, d), row), pl.BlockSpec((1, tm, d), row)],
        out_shape=[jax.ShapeDtypeStruct((bn, l, d), F32), jax.ShapeDtypeStruct((bn, l, d), BF16)],
        compiler_params=_params(("parallel", "parallel"), 24 * tm * d * 4),
        name="ml_out_proj",
    )(hf, hb, o, ng, w, b, x, g1, lng, lnb, sh2, sc2)


def _ml_in_kernel(qk_w, v_w, x_ref, shl_ref, scl_ref, shc_ref, scc_ref, w_ref, b_ref, wg_ref, bg_ref,
                  q_ref, k_ref, v_ref, o_ref, g_ref):
    is_ctx = pl.program_id(1) == 0
    sh = jnp.where(is_ctx, shc_ref[0], shl_ref[0])
    sc = jnp.where(is_ctx, scc_ref[0], scl_ref[0])
    h = (_ln(x_ref[0]) * (1.0 + sc) + sh).astype(BF16)
    p = _dot(h, w_ref[...]) + b_ref[...]
    dqk = qk_w // ML_HEADS
    q_ref[0] = p[:, :qk_w].astype(BF16)
    k_ref[0] = (p[:, qk_w:2 * qk_w] * (dqk ** -0.5)).astype(BF16)
    v_ref[0] = p[:, 2 * qk_w:2 * qk_w + v_w].astype(BF16)
    o_ref[0] = p[:, 2 * qk_w + v_w:]
    g_ref[0] = (_dot(h, wg_ref[...]) + bg_ref[...])[:, :g_ref.shape[2]]


def _ml_in(xcat, lc, shl, scl, shc, scc, w, b, wg, bg, qk_w, v_w):
    bn, s, d = xcat.shape
    n = w.shape[1]
    ng = 4 * ML_HEADS
    tm = lc
    row = lambda i, j: (i, j, 0)
    mod = lambda i, j: (i, 0, 0)
    one = lambda i, j: (0, 0, 0)
    fix = lambda i, j: (0, 0)
    kern = functools.partial(_ml_in_kernel, qk_w, v_w)
    return pl.pallas_call(
        kern,
        grid=(bn, s // tm),
        in_specs=[pl.BlockSpec((1, tm, d), row), pl.BlockSpec((1, 1, d), mod), pl.BlockSpec((1, 1, d), mod),
                  pl.BlockSpec((1, 1, d), one), pl.BlockSpec((1, 1, d), one),
                  pl.BlockSpec((d, n), fix), pl.BlockSpec((1, n), fix),
                  pl.BlockSpec(wg.shape, fix), pl.BlockSpec(bg.shape, fix)],
        out_specs=[pl.BlockSpec((1, tm, qk_w), row), pl.BlockSpec((1, tm, qk_w), row),
                   pl.BlockSpec((1, tm, v_w), row), pl.BlockSpec((1, tm, v_w), row),
                   pl.BlockSpec((1, tm, ng), row)],
        out_shape=[jax.ShapeDtypeStruct((bn, s, qk_w), BF16), jax.ShapeDtypeStruct((bn, s, qk_w), BF16),
                   jax.ShapeDtypeStruct((bn, s, v_w), BF16), jax.ShapeDtypeStruct((bn, s, v_w), F32),
                   jax.ShapeDtypeStruct((bn, s, ng), F32)],
        compiler_params=_params(("parallel", "parallel"), 2 * (tm * d * 4 + d * n * 2) + 8 * tm * n * 4),
        name="ml_in_proj",
    )(xcat, shl, scl, shc, scc, w, b, wg, bg)


def _log_sigmoid(x):
    return jnp.minimum(x, 0.0) - jnp.log1p(jnp.exp(-jnp.abs(x)))


def _mlstm_kernel(qf_ref, kf_ref, vf_ref, gf_ref, gtf_ref, qb_ref, kb_ref, vb_ref, gb_ref, gtb_ref,
                  hf_ref, hb_ref, ct_ref, n_ref, m_ref):
    @pl.when(pl.program_id(1) == 0)
    def _():
        ct_ref[...] = jnp.zeros_like(ct_ref)
        n_ref[...] = jnp.zeros_like(n_ref)
        m_ref[...] = jnp.zeros_like(m_ref)

    ch = qf_ref.shape[1]
    dk = qf_ref.shape[2] // ML_HEADS
    dv = vf_ref.shape[2] // ML_HEADS
    row = lax.broadcasted_iota(jnp.int32, (ch, ch), 0)
    col = lax.broadcasted_iota(jnp.int32, (ch, ch), 1)
    chains = []
    for backward, q_ref, k_ref, v_ref, g_ref, gt_ref, h_ref in (
            (False, qf_ref, kf_ref, vf_ref, gf_ref, gtf_ref, hf_ref),
            (True, qb_ref, kb_ref, vb_ref, gb_ref, gtb_ref, hb_ref)):
        seen = (col >= row) if backward else (col <= row)
        g = g_ref[0, 0]
        gt = gt_ref[0, 0, 0]
        f_cols = _log_sigmoid(g[:, ML_HEADS:])
        f_rows = _log_sigmoid(gt[ML_HEADS:, :])
        seen_f = seen.astype(F32)
        b_cols = jnp.dot(seen_f, f_cols, precision=lax.Precision.HIGHEST, preferred_element_type=F32)
        b_rows = lax.dot_general(f_rows, seen_f, (((1,), (1,)), ((), ())), precision=lax.Precision.HIGHEST,
                                 preferred_element_type=F32)
        for head in range(ML_HEADS):
            chains.append(dict(
                st=(ML_HEADS if backward else 0) + head, head=head, seen=seen, h_ref=h_ref,
                i_col=g[:, head:head + 1], i_row=gt[head:head + 1, :], f_row=f_rows[head:head + 1, :],
                b_col=b_cols[:, head:head + 1], b_row=b_rows[head:head + 1, :],
                q=q_ref[0, :, head * dk:(head + 1) * dk], k=k_ref[0, :, head * dk:(head + 1) * dk],
                v=v_ref[0, :, head * dv:(head + 1) * dv]))
    for c in chains:
        c["m"] = m_ref[c["st"]]
        c["a_col"] = c["b_col"] + c["m"]
        c["dlog"] = jnp.where(c["seen"], c["b_col"] - c["b_row"] + c["i_row"], NEG_INF)
        c["qk"] = _dot_nt(c["q"], c["k"])
        c["ct"] = ct_ref[c["st"]]
        c["qc"] = _dot(c["q"], c["ct"].astype(BF16))
    for c in chains:
        c["mt"] = jnp.maximum(c["a_col"], jnp.max(c["dlog"], axis=1, keepdims=True))
        c["qn"] = jnp.sum(c["q"].astype(F32) * n_ref[c["st"]], axis=1, keepdims=True)
        c["bl"] = jnp.sum(c["f_row"], axis=1, keepdims=True)
        c["wl"] = c["bl"] - c["b_col"] + c["i_col"]
    for c in chains:
        c["smat"] = c["qk"] * jnp.exp(c["dlog"] - c["mt"])
        c["aw"] = jnp.exp(c["a_col"] - c["mt"])
        c["mn"] = jnp.maximum(c["bl"] + c["m"], jnp.max(c["wl"], axis=0, keepdims=True))
    for c in chains:
        c["sv"] = _dot(c["smat"].astype(BF16), c["v"])
        c["den"] = jnp.sum(c["smat"], axis=1, keepdims=True) + c["aw"] * c["qn"]
        wc = jnp.exp(c["wl"] - c["mn"])
        c["wc"] = wc
        c["kv"] = _dot_tn(c["k"], (wc * c["v"].astype(F32)).astype(BF16))
    for c in chains:
        head, st = c["head"], c["st"]
        num = c["sv"] + c["aw"] * c["qc"]
        c["h_ref"][0, :, head * dv:(head + 1) * dv] = num / jnp.maximum(jnp.abs(c["den"]), jnp.exp(-c["mt"]))
        dc = jnp.exp(c["bl"] + c["m"] - c["mn"])
        ct_ref[st] = dc * c["ct"] + c["kv"]
        n_ref[st] = dc * n_ref[st] + jnp.sum(c["wc"] * c["k"].astype(F32), axis=0, keepdims=True)
        m_ref[st] = c["mn"]


def _mlstm(q, k, v, g2, gt2, n_ctx_chunks):
    bn, s, qk_w = q.shape
    v_w = v.shape[2]
    ch = ML_CHUNK
    nc = s // ch
    ncx = n_ctx_chunks

    def back(t):
        return jnp.where(t < ncx, ncx - 1 - t, nc + ncx - 1 - t)

    def specs(d, chunk):
        seq = lambda b, t: (b, chunk(t), 0)
        return [pl.BlockSpec((1, ch, qk_w), seq), pl.BlockSpec((1, ch, qk_w), seq), pl.BlockSpec((1, ch, v_w), seq),
                pl.BlockSpec((1, 1, ch, 2 * ML_HEADS), lambda b, t: (d, b, chunk(t), 0)),
                pl.BlockSpec((1, 1, 1, 2 * ML_HEADS, ch), lambda b, t: (d, b, chunk(t), 0, 0))]

    fwd = lambda t: t
    h_shape = jax.ShapeDtypeStruct((bn, s, v_w), F32)
    states = 2 * ML_HEADS
    return pl.pallas_call(
        _mlstm_kernel,
        grid=(bn, nc),
        in_specs=specs(0, fwd) + specs(1, back),
        out_specs=[pl.BlockSpec((1, ch, v_w), lambda b, t: (b, t, 0)),
                   pl.BlockSpec((1, ch, v_w), lambda b, t: (b, back(t), 0))],
        out_shape=[h_shape, h_shape],
        scratch_shapes=[pltpu.VMEM((states, qk_w // ML_HEADS, v_w // ML_HEADS), F32),
                        pltpu.VMEM((states, 1, qk_w // ML_HEADS), F32),
                        pltpu.VMEM((states, 1, 1), F32)],
        compiler_params=_params(("parallel", "arbitrary"), 16 * 1024 * 1024),
        name="mlstm_scan",
    )(q, k, v, g2, gt2, q, k, v, g2, gt2)


def _sort_network(lo, hi):
    def merge(lo, hi, r):
        step = 2 * r
        if step < hi - lo:
            yield from merge(lo, hi, step)
            yield from merge(lo + r, hi, step)
            yield from ((i, i + r) for i in range(lo + r, hi - r, step))
        else:
            yield (lo, lo + r)

    if hi - lo >= 1:
        mid = lo + (hi - lo) // 2
        yield from _sort_network(lo, mid)
        yield from _sort_network(mid + 1, hi)
        yield from merge(lo, hi, 1)


def _bitonic_merge_network(n):
    stride = n // 2
    while stride >= 1:
        yield from ((i, i + stride) for i in range(n) if not i & stride)
        stride //= 2


def _compare_exchange(x, pairs):
    for i, j in pairs:
        x[i], x[j] = jnp.maximum(x[i], x[j]), jnp.minimum(x[i], x[j])


def _top_values(s, vals_ref):
    n = s.shape[0] // V7X_SUBLANES
    assert n == PEER_TOPK
    x = [s[v * V7X_SUBLANES:(v + 1) * V7X_SUBLANES] for v in range(n)]
    _compare_exchange(x, list(_sort_network(0, n - 1)))
    shift = V7X_SUBLANES // 2
    while shift >= 1:
        x = [jnp.maximum(x[i], pltpu.roll(x[n - 1 - i], shift, 0)) for i in range(n)]
        _compare_exchange(x, list(_bitonic_merge_network(n)))
        shift //= 2
    for i in range(n):
        vals_ref[i:i + 1, :] = x[i][0:1]


def _peer_sel_kernel(hp_ref, wqt_ref, k1_ref, k2_ref, r2_ref, e2_ref, kk_ref, cc_ref, v1_ref, v2_ref, cnt_ref):
    half = N_KEYS
    qt = _dot_nt(wqt_ref[...], hp_ref[...])
    for h in range(PEER_HEADS):
        q1 = qt[(2 * h) * half:(2 * h + 1) * half].astype(BF16)
        q2 = qt[(2 * h + 1) * half:(2 * h + 2) * half].astype(BF16)
        s1 = _dot(k1_ref[h], q1)
        s2 = _dot(k2_ref[h], q2)
        _top_values(s1, v1_ref)
        _top_values(s2, v2_ref)
        v1 = v1_ref[...]
        v2 = v2_ref[...]
        cand = jnp.concatenate(
            [v1 + v2[0:1]] + [v1[0:8] + v2[j:j + 1] for j in range(1, 8)] + [v1[0:1] + v2[8:16]], axis=0)
        work = cand
        theta = None
        for _ in range(PEER_TOPK):
            theta = jnp.max(work, axis=0, keepdims=True)
            work = jnp.where(work == theta, NEG_INF, work)
        smax = v1[0:1] + v2[0:1]
        sel = cand >= theta
        z = jnp.sum(jnp.where(sel, jnp.exp(cand - smax), 0.0), axis=0, keepdims=True)
        picked = sel.astype(F32)
        cnt_ref[...] = picked[0:16]
        cnt_ref[0:8, :] += functools.reduce(jnp.add, [picked[8 + 8 * j:16 + 8 * j] for j in range(1, 8)])
        cnt_ref[0:1, :] += jnp.sum(picked[72:80], axis=0, keepdims=True)
        kk = jnp.zeros(s1.shape, F32)
        for i in range(PEER_TOPK):
            kk = jnp.where(s1 == v1_ref[i:i + 1, :], cnt_ref[i:i + 1, :], kk)
        r2 = jnp.full(s2.shape, float(PEER_TOPK), F32)
        for jr in reversed(range(PEER_TOPK)):
            r2 = jnp.where(s2 >= v2_ref[jr:jr + 1, :], float(jr), r2)
        r2 = r2.astype(BF16)
        e2 = jnp.exp(s2 - v2[0:1]).astype(BF16)
        for grp in range(N_KEYS // V7X_BF16_ROWS):
            r2_ref[0, h, grp] = r2[grp * V7X_BF16_ROWS:(grp + 1) * V7X_BF16_ROWS]
            e2_ref[0, h, grp] = e2[grp * V7X_BF16_ROWS:(grp + 1) * V7X_BF16_ROWS]
        kk_ref[0, h] = kk.astype(BF16)
        cc_ref[0, h] = (jnp.exp(s1 - v1[0:1]) / z).astype(BF16)


def _peer_select(hp, wqt, k1, k2, t):
    nt, d = hp.shape
    nb = nt // t
    sel_spec = pl.BlockSpec((1, PEER_HEADS, N_KEYS, t), lambda i: (i, 0, 0, 0))
    sel_shape = jax.ShapeDtypeStruct((nb, PEER_HEADS, N_KEYS, t), BF16)
    groups = N_KEYS // V7X_BF16_ROWS
    pk_spec = pl.BlockSpec((1, PEER_HEADS, groups, V7X_BF16_ROWS, t), lambda i: (i, 0, 0, 0, 0))
    pk_shape = jax.ShapeDtypeStruct((nb, PEER_HEADS, groups, V7X_BF16_ROWS, t), BF16)
    return pl.pallas_call(
        _peer_sel_kernel,
        grid=(nb,),
        in_specs=[pl.BlockSpec((t, d), lambda i: (i, 0)), pl.BlockSpec(wqt.shape, lambda i: (0, 0)),
                  pl.BlockSpec(k1.shape, lambda i: (0, 0, 0)), pl.BlockSpec(k2.shape, lambda i: (0, 0, 0))],
        out_specs=[pk_spec, pk_spec, sel_spec, sel_spec],
        out_shape=[pk_shape, pk_shape, sel_shape, sel_shape],
        scratch_shapes=[pltpu.VMEM((PEER_TOPK, t), F32), pltpu.VMEM((PEER_TOPK, t), F32),
                        pltpu.VMEM((PEER_TOPK, t), F32)],
        compiler_params=_params(("parallel",), 40 * 1024 * 1024),
        name="peer_select",
    )(hp, wqt, k1, k2)


def _gelu(x):
    return 0.5 * x * (1.0 + lax.erf(x * (2.0 ** -0.5)))


def _peer_dense_kernel(hp_ref, u_ref, vt_ref, r2_ref, e2_ref, kk_ref, cc_ref, x1_ref, g2_ref, lng_ref, lnb_ref,
                       x2_ref, acc_ref, w_ref, r2s_ref, e2s_ref, act_ref):
    j = pl.program_id(1)
    tn, t = u_ref.shape[0], hp_ref.shape[0]
    na = tn // N_KEYS

    @pl.when(j == 0)
    def _():
        acc_ref[...] = jnp.zeros_like(acc_ref)
        r2s_ref[...] = r2_ref[0]
        e2s_ref[...] = e2_ref[0]

    a_rows = pl.ds(pl.multiple_of(j * na, na), na)
    tile = (N_KEYS // V7X_BF16_ROWS, V7X_BF16_ROWS, V7X_LANES)
    for ag in range(0, na, PEER_A_GROUP):
        rows = slice(ag * N_KEYS, (ag + PEER_A_GROUP) * N_KEYS)
        act_ref[...] = _gelu(_dot_nt(u_ref[rows, :], hp_ref[...])).astype(BF16).reshape(act_ref.shape)

        for lt in range(t // V7X_LANES):
            lanes = slice(lt * V7X_LANES, (lt + 1) * V7X_LANES)
            gsum = [jnp.zeros(tile, BF16) for _ in range(PEER_A_GROUP)]
            for h in range(PEER_HEADS):
                r2 = r2s_ref[h, :, :, lanes]
                e2 = e2s_ref[h, :, :, lanes]
                kk = kk_ref[0, h, a_rows, lanes].astype(F32)
                cc = cc_ref[0, h, a_rows, lanes].astype(F32)
                for i in range(PEER_A_GROUP):
                    al = ag + i
                    kk_t = jnp.broadcast_to(kk[al:al + 1], tile[1:]).astype(BF16)[None]
                    cc_t = jnp.broadcast_to(cc[al:al + 1], tile[1:]).astype(BF16)[None]
                    gsum[i] = jnp.where(r2 < kk_t, gsum[i] + e2 * cc_t, gsum[i])
            for i in range(PEER_A_GROUP):
                w_ref[(ag + i) * tile[0]:(ag + i + 1) * tile[0], :, lanes] = (
                    gsum[i] * act_ref[i * tile[0]:(i + 1) * tile[0], :, lanes])
    acc_ref[...] += _dot(vt_ref[...], w_ref[...].reshape(tn, t))

    @pl.when(j == pl.num_programs(1) - 1)
    def _():
        y = ALPHA * x1_ref[...] + g2_ref[0] * acc_ref[...].T
        x2_ref[...] = _ln(y) * lng_ref[...] + lnb_ref[...]


def _peer_dense(hp, u, vt, layer, sel, x1, g2, blocks_per_row, lng, lnb, t, tn):
    nt, d = hp.shape
    ne = u.shape[1]
    sel_spec = pl.BlockSpec((1, PEER_HEADS, N_KEYS, t), lambda i, j: (i, 0, 0, 0))
    pk_spec = pl.BlockSpec((1, PEER_HEADS, N_KEYS // V7X_BF16_ROWS, V7X_BF16_ROWS, t), lambda i, j: (i, 0, 0, 0, 0))
    fix = lambda i, j: (0, 0)
    return pl.pallas_call(
        _peer_dense_kernel,
        grid=(nt // t, ne // tn),
        in_specs=[pl.BlockSpec((t, d), lambda i, j: (i, 0)), pl.BlockSpec((None, tn, d), lambda i, j: (layer, j, 0)),
                  pl.BlockSpec((None, d, tn), lambda i, j: (layer, 0, j)), pk_spec, pk_spec, sel_spec, sel_spec,
                  pl.BlockSpec((t, d), lambda i, j: (i, 0)),
                  pl.BlockSpec((1, 1, d), lambda i, j: (i // blocks_per_row, 0, 0)),
                  pl.BlockSpec((1, d), fix), pl.BlockSpec((1, d), fix)],
        out_specs=pl.BlockSpec((t, d), lambda i, j: (i, 0)),
        out_shape=jax.ShapeDtypeStruct((nt, d), F32),
        scratch_shapes=[pltpu.VMEM((d, t), F32), pltpu.VMEM((tn // V7X_BF16_ROWS, V7X_BF16_ROWS, t), BF16),
                        pltpu.VMEM(pk_spec.block_shape[1:], BF16), pltpu.VMEM(pk_spec.block_shape[1:], BF16),
                        pltpu.VMEM((PEER_A_GROUP * N_KEYS // V7X_BF16_ROWS, V7X_BF16_ROWS, t), BF16)],
        compiler_params=_params(("parallel", "arbitrary"), 48 * 1024 * 1024),
        name="peer_dense",
    )(hp, u, vt, *sel, x1, g2, lng, lnb)


def _peer_block(x1, hp, g2, wqt, k1, k2, u, vt, layer, lng, lnb):
    bn, l, d = x1.shape
    t = min(512, l)
    tn = 16 * N_KEYS
    hp2 = hp.reshape(bn * l, d)
    sel = _peer_select(hp2, wqt, k1, k2, t)
    x2 = _peer_dense(hp2, u, vt, layer, sel, x1.reshape(bn * l, d), g2, l // t, lng, lnb, t, tn)
    return x2.reshape(bn, l, d)


def _rope_tables(l, att_w):
    rows = l // GRID_W
    row = jnp.repeat(jnp.arange(rows, dtype=F32), GRID_W)
    col = jnp.tile(jnp.arange(GRID_W, dtype=F32), rows)
    f = HEAD_DIM // 4
    inv_freq = ROPE_BASE ** (-jnp.arange(f, dtype=F32) / f)
    ar = row[:, None] * inv_freq[None, :]
    ac = col[:, None] * inv_freq[None, :]
    cos = jnp.concatenate([jnp.cos(ar), jnp.cos(ar), jnp.cos(ac), jnp.cos(ac)], axis=-1)
    sin = jnp.concatenate([-jnp.sin(ar), jnp.sin(ar), -jnp.sin(ac), jnp.sin(ac)], axis=-1)
    reps = att_w // HEAD_DIM
    return jnp.tile(cos, (1, reps)), jnp.tile(sin, (1, reps))


def _block_diag_ones(width, group):
    idx = jnp.arange(width) // group
    return (idx[:, None] == idx[None, :]).astype(BF16)


def kernel(x, c, ctx, c_ctx, w_mod, b_mod, ln1_g, ln1_b, ln2_g, ln2_b, ab_w_in, ab_b_in, ab_conv_w, ab_conv_b,
           ab_conv_ln_g, ab_conv_ln_b, ab_q_norm_g, ab_k_norm_g, ab_w_out, ab_b_out, ml_w_in, ml_b_in, ml_norm_g,
           ml_w_out, ml_b_out, peer_w_q, peer_k1, peer_k2, peer_u, peer_v):
    bsz, l, d = x.shape
    lc = ctx.shape[1]
    assert l % GRID_W == 0 and lc % ML_CHUNK == 0 and l % lc == 0

    rows = -(-(bsz + 1) // 8) * 8
    c_rows = jnp.zeros((rows, d), F32).at[:bsz].set(c).at[bsz].set(c_ctx)
    mod = _modulation(c_rows, w_mod, b_mod)

    def mod_rows(i):
        m = mod[i].reshape(rows, N_MOD, d)
        lat = [m[:bsz, k][:, None, :] for k in range(N_MOD)]
        cx = [m[bsz:bsz + 1, k][:, None, :] for k in range(N_MOD)]
        return lat, cx

    row2 = lambda v: v.reshape(1, -1)

    conv_ch = ab_conv_w.shape[2]
    kv_w = ATT_KV_HEADS * HEAD_DIM
    att_w = ATT_HEADS * HEAD_DIM
    (sh1l, sc1l, g1l, sh2l, sc2l, g2l), (sh1c, sc1c, g1c, sh2c, sc2c, g2c) = mod_rows(0)
    bcast = lambda v: jnp.broadcast_to(v, (bsz, 1, d))
    w_in = ab_w_in[0].astype(BF16)
    b_in = row2(ab_b_in[0])
    qg = row2(jnp.tile(ab_q_norm_g[0], ATT_HEADS))
    kg = row2(jnp.tile(ab_k_norm_g[0], ATT_KV_HEADS))
    bd = _block_diag_ones(2 * V7X_LANES, HEAD_DIM)
    cos, sin = _rope_tables(l, att_w)
    ul, ql, kl, vl = _ab_in(x, sh1l, sc1l, w_in, b_in, qg, kg, bd, cos, sin, True, conv_ch, att_w, kv_w)
    uc, qc, kc, vc = _ab_in(ctx, bcast(sh1c), bcast(sc1c), w_in, b_in, qg, kg, bd, cos[:lc], sin[:lc], False,
                            conv_ch, att_w, kv_w)
    conv_args = (ab_conv_w[0], row2(ab_conv_b[0]), row2(ab_conv_ln_g[0]), row2(ab_conv_ln_b[0]))
    conv_l = _conv_group(ul, *conv_args)
    conv_c = _conv_group(uc, *conv_args)
    att_l = _attention(ql, jnp.concatenate([kc, kl], axis=1), jnp.concatenate([vc, vl], axis=1))
    att_c = _attention(qc, kc, vc)
    w_out = ab_w_out[0].astype(BF16)
    out_args = (w_out[:conv_ch], w_out[conv_ch:], row2(ab_b_out[0]))
    ln1 = (row2(ln1_g[0]), row2(ln1_b[0]))
    x1, hpl = _ab_out(conv_l, att_l, *out_args, x, g1l, *ln1, sh2l, sc2l)
    c1, hpc = _ab_out(conv_c, att_c, *out_args, ctx, bcast(g1c), *ln1, bcast(sh2c), bcast(sc2c))

    assert peer_k1.shape[3] == N_KEYS
    u_all = peer_u.astype(BF16)
    vt_all = jnp.swapaxes(peer_v, 1, 2).astype(BF16)

    def peer_weights(i):
        wqt = peer_w_q[i].T.astype(BF16)
        return (wqt, peer_k1[i].astype(BF16), peer_k2[i].astype(BF16), u_all, vt_all, i,
                row2(ln2_g[i]), row2(ln2_b[i]))

    pw = peer_weights(0)
    x = _peer_block(x1, hpl, g2l, *pw)
    ctx = _peer_block(c1.reshape(1, bsz * lc, d), hpc.reshape(1, bsz * lc, d), g2c, *pw).reshape(bsz, lc, d)

    (sh1l, sc1l, g1l, sh2l, sc2l, g2l), (sh1c, sc1c, _, _, _, _) = mod_rows(1)
    qk_w = ML_HEADS * (d // 8)
    v_w = ML_HEADS * (d // 4)
    n_main = 2 * qk_w + 2 * v_w
    w_in = ml_w_in[0]
    wg = jnp.zeros((d, V7X_LANES), F32).at[:, :4 * ML_HEADS].set(w_in[:, n_main:]).astype(BF16)
    bg = jnp.zeros((1, V7X_LANES), F32).at[:, :4 * ML_HEADS].set(ml_b_in[0][n_main:])
    xcat = jnp.concatenate([ctx, x], axis=1)
    q, k, v, o, g = _ml_in(xcat, lc, sh1l, sc1l, sh1c, sc1c, w_in[:, :n_main].astype(BF16),
                           row2(ml_b_in[0][:n_main]), wg, bg, qk_w, v_w)
    s = lc + l
    g2 = g.reshape(bsz, s, 2, 2 * ML_HEADS).transpose(2, 0, 1, 3)
    gt2 = g2.reshape(2, bsz, s // ML_CHUNK, ML_CHUNK, 2 * ML_HEADS).transpose(0, 1, 2, 4, 3)
    hf, hb = _mlstm(q, k, v, g2, gt2, lc // ML_CHUNK)
    x1, hpl = _ml_out(hf, hb, o, lc, row2(ml_norm_g[0]), ml_w_out[0].astype(BF16), row2(ml_b_out[0]), x, g1l,
                      row2(ln1_g[1]), row2(ln1_b[1]), sh2l, sc2l)
    return _peer_block(x1, hpl, g2l, *peer_weights(1))
```

```python
import functools
import math

import jax
import jax.numpy as jnp
from jax import lax
from jax.experimental import pallas as pl
from jax.experimental.pallas import tpu as pltpu

F32 = jnp.float32
BF16 = jnp.bfloat16

DEPTH = 2
EPS = 1e-6
ALPHA = (2 * DEPTH) ** 0.25
N_MOD = 6
GRID_W = 64

CONV_WIDTH = 31
ATT_HEADS = 8
ATT_KV_HEADS = 2
HEAD_DIM = 64
ROPE_BASE = 10000.0

ML_HEADS = 4
ML_CHUNK = 64
ML_BATCH_TILE = 2

PEER_HEADS = 8
N_KEYS = 128
PEER_TOPK = 16
PEER_A_GROUP = 4

V7X_VMEM_BYTES = 64 * 1024 * 1024
V7X_LANES = 128
V7X_SUBLANES = 8
V7X_BF16_ROWS = 16
NEG_INF = float("-inf")


def _params(semantics, vmem_bytes):
    return pltpu.CompilerParams(dimension_semantics=semantics,
                                vmem_limit_bytes=min(int(vmem_bytes), V7X_VMEM_BYTES - 8 * 1024 * 1024))


def _ln(x):
    mu = jnp.mean(x, axis=-1, keepdims=True)
    xc = x - mu
    var = jnp.mean(xc * xc, axis=-1, keepdims=True)
    return xc * lax.rsqrt(var + EPS)


def _dot(a, b):
    return jnp.dot(a, b, preferred_element_type=F32)


def _dot_nt(a, b):
    return lax.dot_general(a, b, (((1,), (1,)), ((), ())), preferred_element_type=F32)


def _dot_tn(a, b):
    return lax.dot_general(a, b, (((0,), (0,)), ((), ())), preferred_element_type=F32)


def _mod_kernel(c_ref, w_ref, b_ref, o_ref):
    c = c_ref[...]
    s = (c * jax.nn.sigmoid(c)).astype(BF16)
    o_ref[0] = _dot(s, w_ref[0].astype(BF16)) + b_ref[0]


def _modulation(c_rows, w_mod, b_mod):
    depth, d, n = w_mod.shape
    rows = c_rows.shape[0]
    tn = 1536
    return pl.pallas_call(
        _mod_kernel,
        grid=(depth, n // tn),
        in_specs=[pl.BlockSpec((rows, d), lambda i, j: (0, 0)),
                  pl.BlockSpec((1, d, tn), lambda i, j: (i, 0, j)),
                  pl.BlockSpec((1, 1, tn), lambda i, j: (i, 0, j))],
        out_specs=pl.BlockSpec((1, rows, tn), lambda i, j: (i, 0, j)),
        out_shape=jax.ShapeDtypeStruct((depth, rows, n), F32),
        compiler_params=_params(("parallel", "parallel"), 4 * d * tn * 4),
        name="modulation",
    )(c_rows, w_mod, b_mod.reshape(depth, 1, n))


def _rope(x, cos, sin_signed):
    w = x.shape[-1]
    lane = lax.broadcasted_iota(jnp.int32, x.shape, 1)
    first = (lane % 32) < 16
    partner = jnp.where(first, pltpu.roll(x, w - 16, 1), pltpu.roll(x, 16, 1))
    return x * cos + partner * sin_signed


def _group_rms(x, bd, g):
    w = bd.shape[0]
    parts = []
    for s in range(0, x.shape[-1], w):
        xs = x[:, s:s + w]
        ms = _dot((xs * xs).astype(BF16), bd) * (1.0 / HEAD_DIM)
        parts.append(xs * lax.rsqrt(ms + EPS))
    y = parts[0] if len(parts) == 1 else jnp.concatenate(parts, axis=-1)
    return y * g


def _ab_in_kernel(use_rope, conv_ch, att_w, kv_w, x_ref, sh_ref, sc_ref, w_ref, b_ref, qg_ref, kg_ref, bd_ref,
                  cos_ref, sin_ref, u_ref, q_ref, k_ref, v_ref):
    h = _ln(x_ref[0]) * (1.0 + sc_ref[0]) + sh_ref[0]
    p = _dot(h.astype(BF16), w_ref[...]) + b_ref[...]
    c1, c2, c3, c4 = conv_ch, 2 * conv_ch, 2 * conv_ch + att_w, 2 * conv_ch + att_w + kv_w
    u_ref[0] = p[:, :c1] * jax.nn.sigmoid(p[:, c1:c2])
    q = _group_rms(p[:, c2:c3], bd_ref[...], qg_ref[...])
    k = _group_rms(p[:, c3:c4], bd_ref[:kv_w, :kv_w], kg_ref[...])
    if use_rope:
        q = _rope(q, cos_ref[...], sin_ref[...])
        k = _rope(k, cos_ref[:, :kv_w], sin_ref[:, :kv_w])
    q_ref[0] = (q * (HEAD_DIM ** -0.5)).astype(BF16)
    k_ref[0] = k.astype(BF16)
    v_ref[0] = p[:, c4:].astype(BF16)


def _ab_in(x, sh, sc, w, b, qg, kg, bd, cos, sin, use_rope, conv_ch, att_w, kv_w):
    bn, l, d = x.shape
    n = w.shape[1]
    tm = min(512, l)
    kern = functools.partial(_ab_in_kernel, use_rope, conv_ch, att_w, kv_w)
    row = lambda i, j: (i, j, 0)
    mod = lambda i, j: (i, 0, 0)
    fix = lambda i, j: (0, 0)
    return pl.pallas_call(
        kern,
        grid=(bn, l // tm),
        in_specs=[pl.BlockSpec((1, tm, d), row), pl.BlockSpec((1, 1, d), mod), pl.BlockSpec((1, 1, d), mod),
                  pl.BlockSpec((d, n), fix), pl.BlockSpec((1, n), fix),
                  pl.BlockSpec((1, att_w), fix), pl.BlockSpec((1, kv_w), fix), pl.BlockSpec(bd.shape, fix),
                  pl.BlockSpec((tm, att_w), lambda i, j: (j, 0)), pl.BlockSpec((tm, att_w), lambda i, j: (j, 0))],
        out_specs=[pl.BlockSpec((1, tm, conv_ch), row), pl.BlockSpec((1, tm, att_w), row),
                   pl.BlockSpec((1, tm, kv_w), row), pl.BlockSpec((1, tm, kv_w), row)],
        out_shape=[jax.ShapeDtypeStruct((bn, l, conv_ch), F32), jax.ShapeDtypeStruct((bn, l, att_w), BF16),
                   jax.ShapeDtypeStruct((bn, l, kv_w), BF16), jax.ShapeDtypeStruct((bn, l, kv_w), BF16)],
        compiler_params=_params(("parallel", "parallel"), 2 * (tm * d * 4 + d * n * 2) + 8 * tm * n * 4),
        name="ab_in_proj",
    )(x, sh, sc, w, b, qg, kg, bd, cos, sin)


def _attn_kernel(q_ref, k_ref, v_ref, o_ref):
    group = ATT_HEADS // ATT_KV_HEADS
    for kh in range(ATT_KV_HEADS):
        kk = k_ref[0, :, kh * HEAD_DIM:(kh + 1) * HEAD_DIM]
        vv = v_ref[0, :, kh * HEAD_DIM:(kh + 1) * HEAD_DIM]
        for g in range(group):
            lo = (kh * group + g) * HEAD_DIM
            s = _dot_nt(q_ref[0, :, lo:lo + HEAD_DIM], kk)
            p = jnp.exp(s - jnp.max(s, axis=-1, keepdims=True))
            denom = jnp.sum(p, axis=-1, keepdims=True)
            o_ref[0, :, lo:lo + HEAD_DIM] = _dot(p.astype(BF16), vv) / denom


def _attention(q, k, v):
    bn, lq, w = q.shape
    s, kvw = k.shape[1], k.shape[2]
    tq = min(256, lq)
    return pl.pallas_call(
        _attn_kernel,
        grid=(bn, lq // tq),
        in_specs=[pl.BlockSpec((1, tq, w), lambda i, j: (i, j, 0)),
                  pl.BlockSpec((1, s, kvw), lambda i, j: (i, 0, 0)),
                  pl.BlockSpec((1, s, kvw), lambda i, j: (i, 0, 0))],
        out_specs=pl.BlockSpec((1, tq, w), lambda i, j: (i, j, 0)),
        out_shape=jax.ShapeDtypeStruct((bn, lq, w), F32),
        compiler_params=_params(("parallel", "parallel"), 8 * tq * s * 4 + 8 * s * kvw * 2),
        name="gqa_attention",
    )(q, k, v)


CONV_ROWS = 32
CONV_HALO = 16


def _conv_kernel(u_ref, w_ref, cb_ref, g_ref, b_ref, o_ref, pad_ref, shift_ref):
    l, ch = u_ref.shape[1], u_ref.shape[2]
    zeros = jnp.zeros((CONV_HALO, ch), F32)
    pad_ref[0:CONV_HALO, :] = zeros
    pad_ref[CONV_HALO + l:2 * CONV_HALO + l, :] = zeros
    pad_ref[CONV_HALO:CONV_HALO + l, :] = u_ref[0]
    first = CONV_HALO - CONV_WIDTH // 2

    def tile(r, carry):
        base = pl.multiple_of(r * CONV_ROWS, CONV_ROWS)
        acc = jnp.zeros((CONV_ROWS, ch), F32)
        win = pad_ref[pl.ds(base, CONV_ROWS + 2 * CONV_HALO), :]
        for res in range(V7X_SUBLANES):
            taps = [j for j in range(CONV_WIDTH) if (first + j) % V7X_SUBLANES == res]
            span = max(first + j - res for j in taps) + CONV_ROWS
            shift_ref[0:span, :] = win[res:res + span, :]
            for j in taps:
                off = first + j - res
                acc = acc + w_ref[j:j + 1, :] * shift_ref[off:off + CONV_ROWS, :]
        y = _ln(acc + cb_ref[...]) * g_ref[...] + b_ref[...]
        o_ref[0, pl.ds(base, CONV_ROWS), :] = y * jax.nn.sigmoid(y)
        return carry

    lax.fori_loop(0, l // CONV_ROWS, tile, 0)


def _conv_group(u, w, cb, g, b):
    bn, l, ch = u.shape
    fix = lambda i: (0, 0)
    return pl.pallas_call(
        _conv_kernel,
        grid=(bn,),
        in_specs=[pl.BlockSpec((1, l, ch), lambda i: (i, 0, 0)), pl.BlockSpec((CONV_WIDTH, ch), fix),
                  pl.BlockSpec((1, ch), fix), pl.BlockSpec((1, ch), fix), pl.BlockSpec((1, ch), fix)],
        out_specs=pl.BlockSpec((1, l, ch), lambda i: (i, 0, 0)),
        out_shape=jax.ShapeDtypeStruct((bn, l, ch), F32),
        scratch_shapes=[pltpu.VMEM((l + 2 * CONV_HALO, ch), F32), pltpu.VMEM((CONV_ROWS + 2 * CONV_HALO, ch), F32)],
        compiler_params=_params(("parallel",), 6 * l * ch * 4),
        name="conv_group",
    )(u, w, cb, g, b)


def _resid_epilogue(y, x_ref, g1_ref, lng_ref, lnb_ref, sh2_ref, sc2_ref, x1_ref, hp_ref):
    x1 = _ln(ALPHA * x_ref[0] + g1_ref[0] * y) * lng_ref[...] + lnb_ref[...]
    x1_ref[0] = x1
    hp_ref[0] = (_ln(x1) * (1.0 + sc2_ref[0]) + sh2_ref[0]).astype(BF16)


def _ab_out_kernel(a_ref, t_ref, w1_ref, w2_ref, b_ref, x_ref, g1_ref, lng_ref, lnb_ref, sh2_ref, sc2_ref,
                   x1_ref, hp_ref):
    y = _dot(a_ref[0].astype(BF16), w1_ref[...]) + _dot(t_ref[0].astype(BF16), w2_ref[...]) + b_ref[...]
    _resid_epilogue(y, x_ref, g1_ref, lng_ref, lnb_ref, sh2_ref, sc2_ref, x1_ref, hp_ref)


def _ab_out(conv, att, w1, w2, b, x, g1, lng, lnb, sh2, sc2):
    bn, l, d = x.shape
    tm = min(512, l)
    row = lambda i, j: (i, j, 0)
    mod = lambda i, j: (i, 0, 0)
    fix = lambda i, j: (0, 0)
    k1, k2 = conv.shape[2], att.shape[2]
    return pl.pallas_call(
        _ab_out_kernel,
        grid=(bn, l // tm),
        in_specs=[pl.BlockSpec((1, tm, k1), row), pl.BlockSpec((1, tm, k2), row),
                  pl.BlockSpec((k1, d), fix), pl.BlockSpec((k2, d), fix), pl.BlockSpec((1, d), fix),
                  pl.BlockSpec((1, tm, d), row), pl.BlockSpec((1, 1, d), mod),
                  pl.BlockSpec((1, d), fix), pl.BlockSpec((1, d), fix),
                  pl.BlockSpec((1, 1, d), mod), pl.BlockSpec((1, 1, d), mod)],
        out_specs=[pl.BlockSpec((1, tm, d), row), pl.BlockSpec((1, tm, d), row)],
        out_shape=[jax.ShapeDtypeStruct((bn, l, d), F32), jax.ShapeDtypeStruct((bn, l, d), BF16)],
        compiler_params=_params(("parallel", "parallel"), 16 * tm * d * 4),
        name="ab_out_proj",
    )(conv, att, w1, w2, b, x, g1, lng, lnb, sh2, sc2)


def _ml_out_kernel(hf_ref, hb_ref, o_ref, ng_ref, w_ref, b_ref, x_ref, g1_ref, lng_ref, lnb_ref, sh2_ref, sc2_ref,
                   x1_ref, hp_ref):
    dv = hf_ref.shape[2] // ML_HEADS
    hs = hf_ref[0] + hb_ref[0]
    parts = [_ln(hs[:, h * dv:(h + 1) * dv]) for h in range(ML_HEADS)]
    hn = jnp.concatenate(parts, axis=-1) * ng_ref[...]
    y = _dot((hn * jax.nn.sigmoid(o_ref[0])).astype(BF16), w_ref[...]) + b_ref[...]
    _resid_epilogue(y, x_ref, g1_ref, lng_ref, lnb_ref, sh2_ref, sc2_ref, x1_ref, hp_ref)


def _ml_out(hf, hb, o, seq_off, ng, w, b, x, g1, lng, lnb, sh2, sc2):
    bn, l, d = x.shape
    vw = o.shape[2]
    tm = math.gcd(256, math.gcd(l, seq_off))
    off = seq_off // tm
    row = lambda i, j: (i, j, 0)
    mod = lambda i, j: (i, 0, 0)
    fix = lambda i, j: (0, 0)
    return pl.pallas_call(
        _ml_out_kernel,
        grid=(bn, l // tm),
        in_specs=[pl.BlockSpec((1, tm, vw), lambda i, j: (i, j + off, 0)),
                  pl.BlockSpec((1, tm, vw), lambda i, j: (i, j + off, 0)),
                  pl.BlockSpec((1, tm, vw), lambda i, j: (i, j + off, 0)),
                  pl.BlockSpec((1, vw), fix), pl.BlockSpec((vw, d), fix), pl.BlockSpec((1, d), fix),
                  pl.BlockSpec((1, tm, d), row), pl.BlockSpec((1, 1, d), mod),
                  pl.BlockSpec((1, d), fix), pl.BlockSpec((1, d), fix),
                  pl.BlockSpec((1, 1, d), mod), pl.BlockSpec((1, 1, d), mod)],
        out_specs=[pl.BlockSpec((1, tm, d), row), pl.BlockSpec((1, tm, d), row)],
        out_shape=[jax.ShapeDtypeStruct((bn, l, d), F32), jax.ShapeDtypeStruct((bn, l, d), BF16)],
        compiler_params=_params(("parallel", "parallel"), 24 * tm * d * 4),
        name="ml_out_proj",
    )(hf, hb, o, ng, w, b, x, g1, lng, lnb, sh2, sc2)


def _ml_in_kernel(qk_w, v_w, x_ref, shl_ref, scl_ref, shc_ref, scc_ref, w_ref, b_ref, wg_ref, bg_ref,
                  q_ref, k_ref, v_ref, o_ref, g_ref):
    is_ctx = pl.program_id(1) == 0
    sh = jnp.where(is_ctx, shc_ref[0], shl_ref[0])
    sc = jnp.where(is_ctx, scc_ref[0], scl_ref[0])
    h = (_ln(x_ref[0]) * (1.0 + sc) + sh).astype(BF16)
    p = _dot(h, w_ref[...]) + b_ref[...]
    dqk = qk_w // ML_HEADS
    q_ref[0] = p[:, :qk_w].astype(BF16)
    k_ref[0] = (p[:, qk_w:2 * qk_w] * (dqk ** -0.5)).astype(BF16)
    v_ref[0] = p[:, 2 * qk_w:2 * qk_w + v_w].astype(BF16)
    o_ref[0] = p[:, 2 * qk_w + v_w:]
    g_ref[0] = (_dot(h, wg_ref[...]) + bg_ref[...])[:, :g_ref.shape[2]]


def _ml_in(xcat, lc, shl, scl, shc, scc, w, b, wg, bg, qk_w, v_w):
    bn, s, d = xcat.shape
    n = w.shape[1]
    ng = 4 * ML_HEADS
    tm = lc
    row = lambda i, j: (i, j, 0)
    mod = lambda i, j: (i, 0, 0)
    one = lambda i, j: (0, 0, 0)
    fix = lambda i, j: (0, 0)
    kern = functools.partial(_ml_in_kernel, qk_w, v_w)
    return pl.pallas_call(
        kern,
        grid=(bn, s // tm),
        in_specs=[pl.BlockSpec((1, tm, d), row), pl.BlockSpec((1, 1, d), mod), pl.BlockSpec((1, 1, d), mod),
                  pl.BlockSpec((1, 1, d), one), pl.BlockSpec((1, 1, d), one),
                  pl.BlockSpec((d, n), fix), pl.BlockSpec((1, n), fix),
                  pl.BlockSpec(wg.shape, fix), pl.BlockSpec(bg.shape, fix)],
        out_specs=[pl.BlockSpec((1, tm, qk_w), row), pl.BlockSpec((1, tm, qk_w), row),
                   pl.BlockSpec((1, tm, v_w), row), pl.BlockSpec((1, tm, v_w), row),
                   pl.BlockSpec((1, tm, ng), row)],
        out_shape=[jax.ShapeDtypeStruct((bn, s, qk_w), BF16), jax.ShapeDtypeStruct((bn, s, qk_w), BF16),
                   jax.ShapeDtypeStruct((bn, s, v_w), BF16), jax.ShapeDtypeStruct((bn, s, v_w), F32),
                   jax.ShapeDtypeStruct((bn, s, ng), F32)],
        compiler_params=_params(("parallel", "parallel"), 2 * (tm * d * 4 + d * n * 2) + 8 * tm * n * 4),
        name="ml_in_proj",
    )(xcat, shl, scl, shc, scc, w, b, wg, bg)


def _log_sigmoid(x):
    return jnp.minimum(x, 0.0) - jnp.log1p(jnp.exp(-jnp.abs(x)))


def _mlstm_kernel(qf_ref, kf_ref, vf_ref, gf_ref, gtf_ref, qb_ref, kb_ref, vb_ref, gb_ref, gtb_ref,
                  hf_ref, hb_ref, ct_ref, n_ref, m_ref):
    @pl.when(pl.program_id(1) == 0)
    def _():
        ct_ref[...] = jnp.zeros_like(ct_ref)
        n_ref[...] = jnp.zeros_like(n_ref)
        m_ref[...] = jnp.zeros_like(m_ref)

    rows_per_step, ch = qf_ref.shape[0], qf_ref.shape[1]
    dk = qf_ref.shape[2] // ML_HEADS
    dv = vf_ref.shape[2] // ML_HEADS
    row = lax.broadcasted_iota(jnp.int32, (ch, ch), 0)
    col = lax.broadcasted_iota(jnp.int32, (ch, ch), 1)
    chains = []
    for bi in range(rows_per_step):
        for backward, q_ref, k_ref, v_ref, g_ref, gt_ref, h_ref in (
                (False, qf_ref, kf_ref, vf_ref, gf_ref, gtf_ref, hf_ref),
                (True, qb_ref, kb_ref, vb_ref, gb_ref, gtb_ref, hb_ref)):
            seen = (col >= row) if backward else (col <= row)
            g = g_ref[0, bi]
            gt = gt_ref[0, bi, 0]
            f_cols = _log_sigmoid(g[:, ML_HEADS:])
            f_rows = _log_sigmoid(gt[ML_HEADS:, :])
            seen_f = seen.astype(F32)
            b_cols = jnp.dot(seen_f, f_cols, precision=lax.Precision.HIGHEST, preferred_element_type=F32)
            b_rows = lax.dot_general(f_rows, seen_f, (((1,), (1,)), ((), ())), precision=lax.Precision.HIGHEST,
                                     preferred_element_type=F32)
            for head in range(ML_HEADS):
                chains.append(dict(
                    st=(2 * bi + int(backward)) * ML_HEADS + head, head=head, bi=bi, seen=seen, h_ref=h_ref,
                    i_col=g[:, head:head + 1], i_row=gt[head:head + 1, :], f_row=f_rows[head:head + 1, :],
                    b_col=b_cols[:, head:head + 1], b_row=b_rows[head:head + 1, :],
                    q=q_ref[bi, :, head * dk:(head + 1) * dk], k=k_ref[bi, :, head * dk:(head + 1) * dk],
                    v=v_ref[bi, :, head * dv:(head + 1) * dv]))
    for c in chains:
        c["m"] = m_ref[c["st"]]
        c["a_col"] = c["b_col"] + c["m"]
        c["dlog"] = jnp.where(c["seen"], c["b_col"] - c["b_row"] + c["i_row"], NEG_INF)
        c["qk"] = _dot_nt(c["q"], c["k"])
        c["ct"] = ct_ref[c["st"]]
        c["qc"] = _dot(c["q"], c["ct"].astype(BF16))
    for c in chains:
        c["mt"] = jnp.maximum(c["a_col"], jnp.max(c["dlog"], axis=1, keepdims=True))
        c["qn"] = jnp.sum(c["q"].astype(F32) * n_ref[c["st"]], axis=1, keepdims=True)
        c["bl"] = jnp.sum(c["f_row"], axis=1, keepdims=True)
        c["wl"] = c["bl"] - c["b_col"] + c["i_col"]
    for c in chains:
        c["smat"] = c["qk"] * jnp.exp(c["dlog"] - c["mt"])
        c["aw"] = jnp.exp(c["a_col"] - c["mt"])
        c["mn"] = jnp.maximum(c["bl"] + c["m"], jnp.max(c["wl"], axis=0, keepdims=True))
    for c in chains:
        c["sv"] = _dot(c["smat"].astype(BF16), c["v"])
        c["den"] = jnp.sum(c["smat"], axis=1, keepdims=True) + c["aw"] * c["qn"]
        wc = jnp.exp(c["wl"] - c["mn"])
        c["wc"] = wc
        c["kv"] = _dot_tn(c["k"], (wc * c["v"].astype(F32)).astype(BF16))
    for c in chains:
        head, st = c["head"], c["st"]
        num = c["sv"] + c["aw"] * c["qc"]
        c["h_ref"][c["bi"], :, head * dv:(head + 1) * dv] = (
            num / jnp.maximum(jnp.abs(c["den"]), jnp.exp(-c["mt"])))
        dc = jnp.exp(c["bl"] + c["m"] - c["mn"])
        ct_ref[st] = dc * c["ct"] + c["kv"]
        n_ref[st] = dc * n_ref[st] + jnp.sum(c["wc"] * c["k"].astype(F32), axis=0, keepdims=True)
        m_ref[st] = c["mn"]


def _mlstm(q, k, v, g2, gt2, n_ctx_chunks):
    bn, s, qk_w = q.shape
    v_w = v.shape[2]
    ch = ML_CHUNK
    nc = s // ch
    ncx = n_ctx_chunks

    def back(t):
        return jnp.where(t < ncx, ncx - 1 - t, nc + ncx - 1 - t)

    nb = math.gcd(bn, ML_BATCH_TILE)

    def specs(d, chunk):
        seq = lambda b, t: (b, chunk(t), 0)
        return [pl.BlockSpec((nb, ch, qk_w), seq), pl.BlockSpec((nb, ch, qk_w), seq),
                pl.BlockSpec((nb, ch, v_w), seq),
                pl.BlockSpec((1, nb, ch, 2 * ML_HEADS), lambda b, t: (d, b, chunk(t), 0)),
                pl.BlockSpec((1, nb, 1, 2 * ML_HEADS, ch), lambda b, t: (d, b, chunk(t), 0, 0))]

    fwd = lambda t: t
    h_shape = jax.ShapeDtypeStruct((bn, s, v_w), F32)
    states = nb * 2 * ML_HEADS
    return pl.pallas_call(
        _mlstm_kernel,
        grid=(bn // nb, nc),
        in_specs=specs(0, fwd) + specs(1, back),
        out_specs=[pl.BlockSpec((nb, ch, v_w), lambda b, t: (b, t, 0)),
                   pl.BlockSpec((nb, ch, v_w), lambda b, t: (b, back(t), 0))],
        out_shape=[h_shape, h_shape],
        scratch_shapes=[pltpu.VMEM((states, qk_w // ML_HEADS, v_w // ML_HEADS), F32),
                        pltpu.VMEM((states, 1, qk_w // ML_HEADS), F32),
                        pltpu.VMEM((states, 1, 1), F32)],
        compiler_params=_params(("parallel", "arbitrary"), 16 * 1024 * 1024),
        name="mlstm_scan",
    )(q, k, v, g2, gt2, q, k, v, g2, gt2)


def _sort_network(lo, hi):
    def merge(lo, hi, r):
        step = 2 * r
        if step < hi - lo:
            yield from merge(lo, hi, step)
            yield from merge(lo + r, hi, step)
            yield from ((i, i + r) for i in range(lo + r, hi - r, step))
        else:
            yield (lo, lo + r)

    if hi - lo >= 1:
        mid = lo + (hi - lo) // 2
        yield from _sort_network(lo, mid)
        yield from _sort_network(mid + 1, hi)
        yield from merge(lo, hi, 1)


def _bitonic_merge_network(n):
    stride = n // 2
    while stride >= 1:
        yield from ((i, i + stride) for i in range(n) if not i & stride)
        stride //= 2


def _compare_exchange(x, pairs):
    for i, j in pairs:
        x[i], x[j] = jnp.maximum(x[i], x[j]), jnp.minimum(x[i], x[j])


def _top_values(s, vals_ref):
    n = s.shape[0] // V7X_SUBLANES
    assert n == PEER_TOPK
    x = [s[v * V7X_SUBLANES:(v + 1) * V7X_SUBLANES] for v in range(n)]
    _compare_exchange(x, list(_sort_network(0, n - 1)))
    shift = V7X_SUBLANES // 2
    while shift >= 1:
        x = [jnp.maximum(x[i], pltpu.roll(x[n - 1 - i], shift, 0)) for i in range(n)]
        _compare_exchange(x, list(_bitonic_merge_network(n)))
        shift //= 2
    for i in range(n):
        vals_ref[i:i + 1, :] = x[i][0:1]


def _peer_sel_kernel(hp_ref, wqt_ref, k1_ref, k2_ref, r2_ref, e2_ref, kk_ref, cc_ref, v1_ref, v2_ref, cnt_ref):
    half = N_KEYS
    qt = _dot_nt(wqt_ref[...], hp_ref[...])
    for h in range(PEER_HEADS):
        q1 = qt[(2 * h) * half:(2 * h + 1) * half].astype(BF16)
        q2 = qt[(2 * h + 1) * half:(2 * h + 2) * half].astype(BF16)
        s1 = _dot(k1_ref[h], q1)
        s2 = _dot(k2_ref[h], q2)
        _top_values(s1, v1_ref)
        _top_values(s2, v2_ref)
        v1 = v1_ref[...]
        v2 = v2_ref[...]
        cand = jnp.concatenate(
            [v1 + v2[0:1]] + [v1[0:8] + v2[j:j + 1] for j in range(1, 8)] + [v1[0:1] + v2[8:16]], axis=0)
        work = cand
        theta = None
        for _ in range(PEER_TOPK):
            theta = jnp.max(work, axis=0, keepdims=True)
            work = jnp.where(work == theta, NEG_INF, work)
        smax = v1[0:1] + v2[0:1]
        sel = cand >= theta
        z = jnp.sum(jnp.where(sel, jnp.exp(cand - smax), 0.0), axis=0, keepdims=True)
        picked = sel.astype(F32)
        cnt_ref[...] = picked[0:16]
        cnt_ref[0:8, :] += functools.reduce(jnp.add, [picked[8 + 8 * j:16 + 8 * j] for j in range(1, 8)])
        cnt_ref[0:1, :] += jnp.sum(picked[72:80], axis=0, keepdims=True)
        kk = jnp.zeros(s1.shape, F32)
        for i in range(PEER_TOPK):
            kk = jnp.where(s1 == v1_ref[i:i + 1, :], cnt_ref[i:i + 1, :], kk)
        r2 = jnp.full(s2.shape, float(PEER_TOPK), F32)
        for jr in reversed(range(PEER_TOPK)):
            r2 = jnp.where(s2 >= v2_ref[jr:jr + 1, :], float(jr), r2)
        r2 = r2.astype(BF16)
        e2 = jnp.exp(s2 - v2[0:1]).astype(BF16)
        for grp in range(N_KEYS // V7X_BF16_ROWS):
            r2_ref[0, h, grp] = r2[grp * V7X_BF16_ROWS:(grp + 1) * V7X_BF16_ROWS]
            e2_ref[0, h, grp] = e2[grp * V7X_BF16_ROWS:(grp + 1) * V7X_BF16_ROWS]
        kk_ref[0, h] = kk.astype(BF16)
        cc_ref[0, h] = (jnp.exp(s1 - v1[0:1]) / z).astype(BF16)


def _peer_select(hp, wqt, k1, k2, t):
    nt, d = hp.shape
    nb = nt // t
    sel_spec = pl.BlockSpec((1, PEER_HEADS, N_KEYS, t), lambda i: (i, 0, 0, 0))
    sel_shape = jax.ShapeDtypeStruct((nb, PEER_HEADS, N_KEYS, t), BF16)
    groups = N_KEYS // V7X_BF16_ROWS
    pk_spec = pl.BlockSpec((1, PEER_HEADS, groups, V7X_BF16_ROWS, t), lambda i: (i, 0, 0, 0, 0))
    pk_shape = jax.ShapeDtypeStruct((nb, PEER_HEADS, groups, V7X_BF16_ROWS, t), BF16)
    return pl.pallas_call(
        _peer_sel_kernel,
        grid=(nb,),
        in_specs=[pl.BlockSpec((t, d), lambda i: (i, 0)), pl.BlockSpec(wqt.shape, lambda i: (0, 0)),
                  pl.BlockSpec(k1.shape, lambda i: (0, 0, 0)), pl.BlockSpec(k2.shape, lambda i: (0, 0, 0))],
        out_specs=[pk_spec, pk_spec, sel_spec, sel_spec],
        out_shape=[pk_shape, pk_shape, sel_shape, sel_shape],
        scratch_shapes=[pltpu.VMEM((PEER_TOPK, t), F32), pltpu.VMEM((PEER_TOPK, t), F32),
                        pltpu.VMEM((PEER_TOPK, t), F32)],
        compiler_params=_params(("parallel",), 40 * 1024 * 1024),
        name="peer_select",
    )(hp, wqt, k1, k2)


def _gelu(x):
    return 0.5 * x * (1.0 + lax.erf(x * (2.0 ** -0.5)))


def _peer_dense_kernel(hp_ref, u_ref, vt_ref, r2_ref, e2_ref, kk_ref, cc_ref, x1_ref, g2_ref, lng_ref, lnb_ref,
                       x2_ref, acc_ref, w_ref, r2s_ref, e2s_ref, act_ref):
    j = pl.program_id(1)
    tn, t = u_ref.shape[0], hp_ref.shape[0]
    na = tn // N_KEYS

    @pl.when(j == 0)
    def _():
        acc_ref[...] = jnp.zeros_like(acc_ref)
        r2s_ref[...] = r2_ref[0]
        e2s_ref[...] = e2_ref[0]

    a_rows = pl.ds(pl.multiple_of(j * na, na), na)
    tile = (N_KEYS // V7X_BF16_ROWS, V7X_BF16_ROWS, V7X_LANES)
    for ag in range(0, na, PEER_A_GROUP):
        rows = slice(ag * N_KEYS, (ag + PEER_A_GROUP) * N_KEYS)
        act_ref[...] = _gelu(_dot_nt(u_ref[rows, :], hp_ref[...])).astype(BF16).reshape(act_ref.shape)

        for lt in range(t // V7X_LANES):
            lanes = slice(lt * V7X_LANES, (lt + 1) * V7X_LANES)
            gsum = [jnp.zeros(tile, BF16) for _ in range(PEER_A_GROUP)]
            for h in range(PEER_HEADS):
                r2 = r2s_ref[h, :, :, lanes]
                e2 = e2s_ref[h, :, :, lanes]
                kk = kk_ref[0, h, a_rows, lanes].astype(F32)
                cc = cc_ref[0, h, a_rows, lanes].astype(F32)
                for i in range(PEER_A_GROUP):
                    al = ag + i
                    kk_t = jnp.broadcast_to(kk[al:al + 1], tile[1:]).astype(BF16)[None]
                    cc_t = jnp.broadcast_to(cc[al:al + 1], tile[1:]).astype(BF16)[None]
                    gsum[i] = jnp.where(r2 < kk_t, gsum[i] + e2 * cc_t, gsum[i])
            for i in range(PEER_A_GROUP):
                w_ref[(ag + i) * tile[0]:(ag + i + 1) * tile[0], :, lanes] = (
                    gsum[i] * act_ref[i * tile[0]:(i + 1) * tile[0], :, lanes])
    acc_ref[...] += _dot(vt_ref[...], w_ref[...].reshape(tn, t))

    @pl.when(j == pl.num_programs(1) - 1)
    def _():
        y = ALPHA * x1_ref[...] + g2_ref[0] * acc_ref[...].T
        x2_ref[...] = _ln(y) * lng_ref[...] + lnb_ref[...]


def _peer_dense(hp, u, vt, layer, sel, x1, g2, blocks_per_row, lng, lnb, t, tn):
    nt, d = hp.shape
    ne = u.shape[1]
    sel_spec = pl.BlockSpec((1, PEER_HEADS, N_KEYS, t), lambda i, j: (i, 0, 0, 0))
    pk_spec = pl.BlockSpec((1, PEER_HEADS, N_KEYS // V7X_BF16_ROWS, V7X_BF16_ROWS, t), lambda i, j: (i, 0, 0, 0, 0))
    fix = lambda i, j: (0, 0)
    return pl.pallas_call(
        _peer_dense_kernel,
        grid=(nt // t, ne // tn),
        in_specs=[pl.BlockSpec((t, d), lambda i, j: (i, 0)), pl.BlockSpec((None, tn, d), lambda i, j: (layer, j, 0)),
                  pl.BlockSpec((None, d, tn), lambda i, j: (layer, 0, j)), pk_spec, pk_spec, sel_spec, sel_spec,
                  pl.BlockSpec((t, d), lambda i, j: (i, 0)),
                  pl.BlockSpec((1, 1, d), lambda i, j: (i // blocks_per_row, 0, 0)),
                  pl.BlockSpec((1, d), fix), pl.BlockSpec((1, d), fix)],
        out_specs=pl.BlockSpec((t, d), lambda i, j: (i, 0)),
        out_shape=jax.ShapeDtypeStruct((nt, d), F32),
        scratch_shapes=[pltpu.VMEM((d, t), F32), pltpu.VMEM((tn // V7X_BF16_ROWS, V7X_BF16_ROWS, t), BF16),
                        pltpu.VMEM(pk_spec.block_shape[1:], BF16), pltpu.VMEM(pk_spec.block_shape[1:], BF16),
                        pltpu.VMEM((PEER_A_GROUP * N_KEYS // V7X_BF16_ROWS, V7X_BF16_ROWS, t), BF16)],
        compiler_params=_params(("parallel", "arbitrary"), 48 * 1024 * 1024),
        name="peer_dense",
    )(hp, u, vt, *sel, x1, g2, lng, lnb)


def _peer_block(x1, hp, g2, wqt, k1, k2, u, vt, layer, lng, lnb):
    bn, l, d = x1.shape
    t = min(512, l)
    tn = 16 * N_KEYS
    hp2 = hp.reshape(bn * l, d)
    sel = _peer_select(hp2, wqt, k1, k2, t)
    x2 = _peer_dense(hp2, u, vt, layer, sel, x1.reshape(bn * l, d), g2, l // t, lng, lnb, t, tn)
    return x2.reshape(bn, l, d)


def _rope_tables(l, att_w):
    rows = l // GRID_W
    row = jnp.repeat(jnp.arange(rows, dtype=F32), GRID_W)
    col = jnp.tile(jnp.arange(GRID_W, dtype=F32), rows)
    f = HEAD_DIM // 4
    inv_freq = ROPE_BASE ** (-jnp.arange(f, dtype=F32) / f)
    ar = row[:, None] * inv_freq[None, :]
    ac = col[:, None] * inv_freq[None, :]
    cos = jnp.concatenate([jnp.cos(ar), jnp.cos(ar), jnp.cos(ac), jnp.cos(ac)], axis=-1)
    sin = jnp.concatenate([-jnp.sin(ar), jnp.sin(ar), -jnp.sin(ac), jnp.sin(ac)], axis=-1)
    reps = att_w // HEAD_DIM
    return jnp.tile(cos, (1, reps)), jnp.tile(sin, (1, reps))


def _block_diag_ones(width, group):
    idx = jnp.arange(width) // group
    return (idx[:, None] == idx[None, :]).astype(BF16)


def kernel(x, c, ctx, c_ctx, w_mod, b_mod, ln1_g, ln1_b, ln2_g, ln2_b, ab_w_in, ab_b_in, ab_conv_w, ab_conv_b,
           ab_conv_ln_g, ab_conv_ln_b, ab_q_norm_g, ab_k_norm_g, ab_w_out, ab_b_out, ml_w_in, ml_b_in, ml_norm_g,
           ml_w_out, ml_b_out, peer_w_q, peer_k1, peer_k2, peer_u, peer_v):
    bsz, l, d = x.shape
    lc = ctx.shape[1]
    assert l % GRID_W == 0 and lc % ML_CHUNK == 0 and l % lc == 0

    rows = -(-(bsz + 1) // 8) * 8
    c_rows = jnp.zeros((rows, d), F32).at[:bsz].set(c).at[bsz].set(c_ctx)
    mod = _modulation(c_rows, w_mod, b_mod)

    def mod_rows(i):
        m = mod[i].reshape(rows, N_MOD, d)
        lat = [m[:bsz, k][:, None, :] for k in range(N_MOD)]
        cx = [m[bsz:bsz + 1, k][:, None, :] for k in range(N_MOD)]
        return lat, cx

    row2 = lambda v: v.reshape(1, -1)

    conv_ch = ab_conv_w.shape[2]
    kv_w = ATT_KV_HEADS * HEAD_DIM
    att_w = ATT_HEADS * HEAD_DIM
    (sh1l, sc1l, g1l, sh2l, sc2l, g2l), (sh1c, sc1c, g1c, sh2c, sc2c, g2c) = mod_rows(0)
    bcast = lambda v: jnp.broadcast_to(v, (bsz, 1, d))
    w_in = ab_w_in[0].astype(BF16)
    b_in = row2(ab_b_in[0])
    qg = row2(jnp.tile(ab_q_norm_g[0], ATT_HEADS))
    kg = row2(jnp.tile(ab_k_norm_g[0], ATT_KV_HEADS))
    bd = _block_diag_ones(2 * V7X_LANES, HEAD_DIM)
    cos, sin = _rope_tables(l, att_w)
    ul, ql, kl, vl = _ab_in(x, sh1l, sc1l, w_in, b_in, qg, kg, bd, cos, sin, True, conv_ch, att_w, kv_w)
    uc, qc, kc, vc = _ab_in(ctx, bcast(sh1c), bcast(sc1c), w_in, b_in, qg, kg, bd, cos[:lc], sin[:lc], False,
                            conv_ch, att_w, kv_w)
    conv_args = (ab_conv_w[0], row2(ab_conv_b[0]), row2(ab_conv_ln_g[0]), row2(ab_conv_ln_b[0]))
    conv_l = _conv_group(ul, *conv_args)
    conv_c = _conv_group(uc, *conv_args)
    att_l = _attention(ql, jnp.concatenate([kc, kl], axis=1), jnp.concatenate([vc, vl], axis=1))
    att_c = _attention(qc, kc, vc)
    w_out = ab_w_out[0].astype(BF16)
    out_args = (w_out[:conv_ch], w_out[conv_ch:], row2(ab_b_out[0]))
    ln1 = (row2(ln1_g[0]), row2(ln1_b[0]))
    x1, hpl = _ab_out(conv_l, att_l, *out_args, x, g1l, *ln1, sh2l, sc2l)
    c1, hpc = _ab_out(conv_c, att_c, *out_args, ctx, bcast(g1c), *ln1, bcast(sh2c), bcast(sc2c))

    assert peer_k1.shape[3] == N_KEYS
    u_all = peer_u.astype(BF16)
    vt_all = jnp.swapaxes(peer_v, 1, 2).astype(BF16)

    def peer_weights(i):
        wqt = peer_w_q[i].T.astype(BF16)
        return (wqt, peer_k1[i].astype(BF16), peer_k2[i].astype(BF16), u_all, vt_all, i,
                row2(ln2_g[i]), row2(ln2_b[i]))

    pw = peer_weights(0)
    x = _peer_block(x1, hpl, g2l, *pw)
    ctx = _peer_block(c1.reshape(1, bsz * lc, d), hpc.reshape(1, bsz * lc, d), g2c, *pw).reshape(bsz, lc, d)

    (sh1l, sc1l, g1l, sh2l, sc2l, g2l), (sh1c, sc1c, _, _, _, _) = mod_rows(1)
    qk_w = ML_HEADS * (d // 8)
    v_w = ML_HEADS * (d // 4)
    n_main = 2 * qk_w + 2 * v_w
    w_in = ml_w_in[0]
    wg = jnp.zeros((d, V7X_LANES), F32).at[:, :4 * ML_HEADS].set(w_in[:, n_main:]).astype(BF16)
    bg = jnp.zeros((1, V7X_LANES), F32).at[:, :4 * ML_HEADS].set(ml_b_in[0][n_main:])
    xcat = jnp.concatenate([ctx, x], axis=1)
    q, k, v, o, g = _ml_in(xcat, lc, sh1l, sc1l, sh1c, sc1c, w_in[:, :n_main].astype(BF16),
                           row2(ml_b_in[0][:n_main]), wg, bg, qk_w, v_w)
    s = lc + l
    g2 = g.reshape(bsz, s, 2, 2 * ML_HEADS).transpose(2, 0, 1, 3)
    gt2 = g2.reshape(2, bsz, s // ML_CHUNK, ML_CHUNK, 2 * ML_HEADS).transpose(0, 1, 2, 4, 3)
    hf, hb = _mlstm(q, k, v, g2, gt2, lc // ML_CHUNK)
    x1, hpl = _ml_out(hf, hb, o, lc, row2(ml_norm_g[0]), ml_w_out[0].astype(BF16), row2(ml_b_out[0]), x, g1l,
                      row2(ln1_g[1]), row2(ln1_b[1]), sh2l, sc2l)
    return _peer_block(x1, hpl, g2l, *peer_weights(1))
```

```python
import functools
import math

import jax
import jax.numpy as jnp
from jax import lax
from jax.experimental import pallas as pl
from jax.experimental.pallas import tpu as pltpu

F32 = jnp.float32
BF16 = jnp.bfloat16

DEPTH = 2
EPS = 1e-6
ALPHA = (2 * DEPTH) ** 0.25
N_MOD = 6
GRID_W = 64

CONV_WIDTH = 31
ATT_HEADS = 8
ATT_KV_HEADS = 2
HEAD_DIM = 64
ROPE_BASE = 10000.0

ML_HEADS = 4
ML_CHUNK = 64
ML_BATCH_TILE = 2

PEER_HEADS = 8
N_KEYS = 128
PEER_TOPK = 16
PEER_A_GROUP = 4

V7X_VMEM_BYTES = 64 * 1024 * 1024
V7X_LANES = 128
V7X_SUBLANES = 8
V7X_BF16_ROWS = 16
NEG_INF = float("-inf")


def _params(semantics, vmem_bytes):
    return pltpu.CompilerParams(dimension_semantics=semantics,
                                vmem_limit_bytes=min(int(vmem_bytes), V7X_VMEM_BYTES - 8 * 1024 * 1024))


def _ln(x):
    mu = jnp.mean(x, axis=-1, keepdims=True)
    xc = x - mu
    var = jnp.mean(xc * xc, axis=-1, keepdims=True)
    return xc * lax.rsqrt(var + EPS)


def _dot(a, b):
    return jnp.dot(a, b, preferred_element_type=F32)


def _dot_nt(a, b):
    return lax.dot_general(a, b, (((1,), (1,)), ((), ())), preferred_element_type=F32)


def _dot_tn(a, b):
    return lax.dot_general(a, b, (((0,), (0,)), ((), ())), preferred_element_type=F32)


def _mod_kernel(c_ref, w_ref, b_ref, o_ref):
    c = c_ref[...]
    s = (c * jax.nn.sigmoid(c)).astype(BF16)
    o_ref[0] = _dot(s, w_ref[0].astype(BF16)) + b_ref[0]


def _modulation(c_rows, w_mod, b_mod):
    depth, d, n = w_mod.shape
    rows = c_rows.shape[0]
    tn = 1536
    return pl.pallas_call(
        _mod_kernel,
        grid=(depth, n // tn),
        in_specs=[pl.BlockSpec((rows, d), lambda i, j: (0, 0)),
                  pl.BlockSpec((1, d, tn), lambda i, j: (i, 0, j)),
                  pl.BlockSpec((1, 1, tn), lambda i, j: (i, 0, j))],
        out_specs=pl.BlockSpec((1, rows, tn), lambda i, j: (i, 0, j)),
        out_shape=jax.ShapeDtypeStruct((depth, rows, n), F32),
        compiler_params=_params(("parallel", "parallel"), 4 * d * tn * 4),
        name="modulation",
    )(c_rows, w_mod, b_mod.reshape(depth, 1, n))


def _rope(x, cos, sin_signed):
    w = x.shape[-1]
    lane = lax.broadcasted_iota(jnp.int32, x.shape, 1)
    first = (lane % 32) < 16
    partner = jnp.where(first, pltpu.roll(x, w - 16, 1), pltpu.roll(x, 16, 1))
    return x * cos + partner * sin_signed


def _group_rms(x, bd, g):
    w = bd.shape[0]
    parts = []
    for s in range(0, x.shape[-1], w):
        xs = x[:, s:s + w]
        ms = _dot((xs * xs).astype(BF16), bd) * (1.0 / HEAD_DIM)
        parts.append(xs * lax.rsqrt(ms + EPS))
    y = parts[0] if len(parts) == 1 else jnp.concatenate(parts, axis=-1)
    return y * g


def _ab_in_kernel(use_rope, conv_ch, att_w, kv_w, x_ref, sh_ref, sc_ref, w_ref, b_ref, qg_ref, kg_ref, bd_ref,
                  cos_ref, sin_ref, u_ref, q_ref, k_ref, v_ref):
    h = _ln(x_ref[0]) * (1.0 + sc_ref[0]) + sh_ref[0]
    p = _dot(h.astype(BF16), w_ref[...]) + b_ref[...]
    c1, c2, c3, c4 = conv_ch, 2 * conv_ch, 2 * conv_ch + att_w, 2 * conv_ch + att_w + kv_w
    u_ref[0] = p[:, :c1] * jax.nn.sigmoid(p[:, c1:c2])
    q = _group_rms(p[:, c2:c3], bd_ref[...], qg_ref[...])
    k = _group_rms(p[:, c3:c4], bd_ref[:kv_w, :kv_w], kg_ref[...])
    if use_rope:
        q = _rope(q, cos_ref[...], sin_ref[...])
        k = _rope(k, cos_ref[:, :kv_w], sin_ref[:, :kv_w])
    q_ref[0] = (q * (HEAD_DIM ** -0.5)).astype(BF16)
    k_ref[0] = k.astype(BF16)
    v_ref[0] = p[:, c4:].astype(BF16)


def _ab_in(x, sh, sc, w, b, qg, kg, bd, cos, sin, use_rope, conv_ch, att_w, kv_w):
    bn, l, d = x.shape
    n = w.shape[1]
    tm = min(512, l)
    kern = functools.partial(_ab_in_kernel, use_rope, conv_ch, att_w, kv_w)
    row = lambda i, j: (i, j, 0)
    mod = lambda i, j: (i, 0, 0)
    fix = lambda i, j: (0, 0)
    return pl.pallas_call(
        kern,
        grid=(bn, l // tm),
        in_specs=[pl.BlockSpec((1, tm, d), row), pl.BlockSpec((1, 1, d), mod), pl.BlockSpec((1, 1, d), mod),
                  pl.BlockSpec((d, n), fix), pl.BlockSpec((1, n), fix),
                  pl.BlockSpec((1, att_w), fix), pl.BlockSpec((1, kv_w), fix), pl.BlockSpec(bd.shape, fix),
                  pl.BlockSpec((tm, att_w), lambda i, j: (j, 0)), pl.BlockSpec((tm, att_w), lambda i, j: (j, 0))],
        out_specs=[pl.BlockSpec((1, tm, conv_ch), row), pl.BlockSpec((1, tm, att_w), row),
                   pl.BlockSpec((1, tm, kv_w), row), pl.BlockSpec((1, tm, kv_w), row)],
        out_shape=[jax.ShapeDtypeStruct((bn, l, conv_ch), F32), jax.ShapeDtypeStruct((bn, l, att_w), BF16),
                   jax.ShapeDtypeStruct((bn, l, kv_w), BF16), jax.ShapeDtypeStruct((bn, l, kv_w), BF16)],
        compiler_params=_params(("parallel", "parallel"), 2 * (tm * d * 4 + d * n * 2) + 8 * tm * n * 4),
        name="ab_in_proj",
    )(x, sh, sc, w, b, qg, kg, bd, cos, sin)


def _attn_kernel(q_ref, k_ref, v_ref, o_ref):
    group = ATT_HEADS // ATT_KV_HEADS
    for kh in range(ATT_KV_HEADS):
        kk = k_ref[0, :, kh * HEAD_DIM:(kh + 1) * HEAD_DIM]
        vv = v_ref[0, :, kh * HEAD_DIM:(kh + 1) * HEAD_DIM]
        for g in range(group):
            lo = (kh * group + g) * HEAD_DIM
            s = _dot_nt(q_ref[0, :, lo:lo + HEAD_DIM], kk)
            p = jnp.exp(s - jnp.max(s, axis=-1, keepdims=True))
            denom = jnp.sum(p, axis=-1, keepdims=True)
            o_ref[0, :, lo:lo + HEAD_DIM] = _dot(p.astype(BF16), vv) / denom


def _attention(q, k, v):
    bn, lq, w = q.shape
    s, kvw = k.shape[1], k.shape[2]
    tq = min(256, lq)
    return pl.pallas_call(
        _attn_kernel,
        grid=(bn, lq // tq),
        in_specs=[pl.BlockSpec((1, tq, w), lambda i, j: (i, j, 0)),
                  pl.BlockSpec((1, s, kvw), lambda i, j: (i, 0, 0)),
                  pl.BlockSpec((1, s, kvw), lambda i, j: (i, 0, 0))],
        out_specs=pl.BlockSpec((1, tq, w), lambda i, j: (i, j, 0)),
        out_shape=jax.ShapeDtypeStruct((bn, lq, w), F32),
        compiler_params=_params(("parallel", "parallel"), 8 * tq * s * 4 + 8 * s * kvw * 2),
        name="gqa_attention",
    )(q, k, v)


CONV_ROWS = 32
CONV_HALO = 16


def _conv_kernel(u_ref, w_ref, cb_ref, g_ref, b_ref, o_ref, pad_ref, shift_ref):
    l, ch = u_ref.shape[1], u_ref.shape[2]
    zeros = jnp.zeros((CONV_HALO, ch), F32)
    pad_ref[0:CONV_HALO, :] = zeros
    pad_ref[CONV_HALO + l:2 * CONV_HALO + l, :] = zeros
    pad_ref[CONV_HALO:CONV_HALO + l, :] = u_ref[0]
    first = CONV_HALO - CONV_WIDTH // 2

    def tile(r, carry):
        base = pl.multiple_of(r * CONV_ROWS, CONV_ROWS)
        acc = jnp.zeros((CONV_ROWS, ch), F32)
        win = pad_ref[pl.ds(base, CONV_ROWS + 2 * CONV_HALO), :]
        for res in range(V7X_SUBLANES):
            taps = [j for j in range(CONV_WIDTH) if (first + j) % V7X_SUBLANES == res]
            span = max(first + j - res for j in taps) + CONV_ROWS
            shift_ref[0:span, :] = win[res:res + span, :]
            for j in taps:
                off = first + j - res
                acc = acc + w_ref[j:j + 1, :] * shift_ref[off:off + CONV_ROWS, :]
        y = _ln(acc + cb_ref[...]) * g_ref[...] + b_ref[...]
        o_ref[0, pl.ds(base, CONV_ROWS), :] = y * jax.nn.sigmoid(y)
        return carry

    lax.fori_loop(0, l // CONV_ROWS, tile, 0)


def _conv_group(u, w, cb, g, b):
    bn, l, ch = u.shape
    fix = lambda i: (0, 0)
    return pl.pallas_call(
        _conv_kernel,
        grid=(bn,),
        in_specs=[pl.BlockSpec((1, l, ch), lambda i: (i, 0, 0)), pl.BlockSpec((CONV_WIDTH, ch), fix),
                  pl.BlockSpec((1, ch), fix), pl.BlockSpec((1, ch), fix), pl.BlockSpec((1, ch), fix)],
        out_specs=pl.BlockSpec((1, l, ch), lambda i: (i, 0, 0)),
        out_shape=jax.ShapeDtypeStruct((bn, l, ch), F32),
        scratch_shapes=[pltpu.VMEM((l + 2 * CONV_HALO, ch), F32), pltpu.VMEM((CONV_ROWS + 2 * CONV_HALO, ch), F32)],
        compiler_params=_params(("parallel",), 6 * l * ch * 4),
        name="conv_group",
    )(u, w, cb, g, b)


def _resid_epilogue(y, x_ref, g1_ref, lng_ref, lnb_ref, sh2_ref, sc2_ref, x1_ref, hp_ref):
    x1 = _ln(ALPHA * x_ref[0] + g1_ref[0] * y) * lng_ref[...] + lnb_ref[...]
    x1_ref[0] = x1
    hp_ref[0] = (_ln(x1) * (1.0 + sc2_ref[0]) + sh2_ref[0]).astype(BF16)


def _ab_out_kernel(a_ref, t_ref, w1_ref, w2_ref, b_ref, x_ref, g1_ref, lng_ref, lnb_ref, sh2_ref, sc2_ref,
                   x1_ref, hp_ref):
    y = _dot(a_ref[0].astype(BF16), w1_ref[...]) + _dot(t_ref[0].astype(BF16), w2_ref[...]) + b_ref[...]
    _resid_epilogue(y, x_ref, g1_ref, lng_ref, lnb_ref, sh2_ref, sc2_ref, x1_ref, hp_ref)


def _ab_out(conv, att, w1, w2, b, x, g1, lng, lnb, sh2, sc2):
    bn, l, d = x.shape
    tm = min(512, l)
    row = lambda i, j: (i, j, 0)
    mod = lambda i, j: (i, 0, 0)
    fix = lambda i, j: (0, 0)
    k1, k2 = conv.shape[2], att.shape[2]
    return pl.pallas_call(
        _ab_out_kernel,
        grid=(bn, l // tm),
        in_specs=[pl.BlockSpec((1, tm, k1), row), pl.BlockSpec((1, tm, k2), row),
                  pl.BlockSpec((k1, d), fix), pl.BlockSpec((k2, d), fix), pl.BlockSpec((1, d), fix),
                  pl.BlockSpec((1, tm, d), row), pl.BlockSpec((1, 1, d), mod),
                  pl.BlockSpec((1, d), fix), pl.BlockSpec((1, d), fix),
                  pl.BlockSpec((1, 1, d), mod), pl.BlockSpec((1, 1, d), mod)],
        out_specs=[pl.BlockSpec((1, tm, d), row), pl.BlockSpec((1, tm, d), row)],
        out_shape=[jax.ShapeDtypeStruct((bn, l, d), F32), jax.ShapeDtypeStruct((bn, l, d), BF16)],
        compiler_params=_params(("parallel", "parallel"), 16 * tm * d * 4),
        name="ab_out_proj",
    )(conv, att, w1, w2, b, x, g1, lng, lnb, sh2, sc2)


def _ml_out_kernel(hf_ref, hb_ref, o_ref, ng_ref, w_ref, b_ref, x_ref, g1_ref, lng_ref, lnb_ref, sh2_ref, sc2_ref,
                   x1_ref, hp_ref):
    dv = hf_ref.shape[2] // ML_HEADS
    hs = hf_ref[0] + hb_ref[0]
    parts = [_ln(hs[:, h * dv:(h + 1) * dv]) for h in range(ML_HEADS)]
    hn = jnp.concatenate(parts, axis=-1) * ng_ref[...]
    y = _dot((hn * jax.nn.sigmoid(o_ref[0])).astype(BF16), w_ref[...]) + b_ref[...]
    _resid_epilogue(y, x_ref, g1_ref, lng_ref, lnb_ref, sh2_ref, sc2_ref, x1_ref, hp_ref)


def _ml_out(hf, hb, o, seq_off, ng, w, b, x, g1, lng, lnb, sh2, sc2):
    bn, l, d = x.shape
    vw = o.shape[2]
    tm = math.gcd(256, math.gcd(l, seq_off))
    off = seq_off // tm
    row = lambda i, j: (i, j, 0)
    mod = lambda i, j: (i, 0, 0)
    fix = lambda i, j: (0, 0)
    return pl.pallas_call(
        _ml_out_kernel,
        grid=(bn, l // tm),
        in_specs=[pl.BlockSpec((1, tm, vw), lambda i, j: (i, j + off, 0)),
                  pl.BlockSpec((1, tm, vw), lambda i, j: (i, j + off, 0)),
                  pl.BlockSpec((1, tm, vw), lambda i, j: (i, j + off, 0)),
                  pl.BlockSpec((1, vw), fix), pl.BlockSpec((vw, d), fix), pl.BlockSpec((1, d), fix),
                  pl.BlockSpec((1, tm, d), row), pl.BlockSpec((1, 1, d), mod),
                  pl.BlockSpec((1, d), fix), pl.BlockSpec((1, d), fix),
                  pl.BlockSpec((1, 1, d), mod), pl.BlockSpec((1, 1, d), mod)],
        out_specs=[pl.BlockSpec((1, tm, d), row), pl.BlockSpec((1, tm, d), row)],
        out_shape=[jax.ShapeDtypeStruct((bn, l, d), F32), jax.ShapeDtypeStruct((bn, l, d), BF16)],
        compiler_params=_params(("parallel", "parallel"), 24 * tm * d * 4),
        name="ml_out_proj",
    )(hf, hb, o, ng, w, b, x, g1, lng, lnb, sh2, sc2)


def _ml_in_kernel(qk_w, v_w, x_ref, shl_ref, scl_ref, shc_ref, scc_ref, w_ref, b_ref, wg_ref, bg_ref,
                  q_ref, k_ref, v_ref, o_ref, g_ref):
    is_ctx = pl.program_id(1) == 0
    sh = jnp.where(is_ctx, shc_ref[0], shl_ref[0])
    sc = jnp.where(is_ctx, scc_ref[0], scl_ref[0])
    h = (_ln(x_ref[0]) * (1.0 + sc) + sh).astype(BF16)
    p = _dot(h, w_ref[...]) + b_ref[...]
    dqk = qk_w // ML_HEADS
    q_ref[0] = p[:, :qk_w].astype(BF16)
    k_ref[0] = (p[:, qk_w:2 * qk_w] * (dqk ** -0.5)).astype(BF16)
    v_ref[0] = p[:, 2 * qk_w:2 * qk_w + v_w].astype(BF16)
    o_ref[0] = p[:, 2 * qk_w + v_w:]
    g_ref[0] = (_dot(h, wg_ref[...]) + bg_ref[...])[:, :g_ref.shape[2]]


def _ml_in(xcat, lc, shl, scl, shc, scc, w, b, wg, bg, qk_w, v_w):
    bn, s, d = xcat.shape
    n = w.shape[1]
    ng = 4 * ML_HEADS
    tm = lc
    row = lambda i, j: (i, j, 0)
    mod = lambda i, j: (i, 0, 0)
    one = lambda i, j: (0, 0, 0)
    fix = lambda i, j: (0, 0)
    kern = functools.partial(_ml_in_kernel, qk_w, v_w)
    return pl.pallas_call(
        kern,
        grid=(bn, s // tm),
        in_specs=[pl.BlockSpec((1, tm, d), row), pl.BlockSpec((1, 1, d), mod), pl.BlockSpec((1, 1, d), mod),
                  pl.BlockSpec((1, 1, d), one), pl.BlockSpec((1, 1, d), one),
                  pl.BlockSpec((d, n), fix), pl.BlockSpec((1, n), fix),
                  pl.BlockSpec(wg.shape, fix), pl.BlockSpec(bg.shape, fix)],
        out_specs=[pl.BlockSpec((1, tm, qk_w), row), pl.BlockSpec((1, tm, qk_w), row),
                   pl.BlockSpec((1, tm, v_w), row), pl.BlockSpec((1, tm, v_w), row),
                   pl.BlockSpec((1, tm, ng), row)],
        out_shape=[jax.ShapeDtypeStruct((bn, s, qk_w), BF16), jax.ShapeDtypeStruct((bn, s, qk_w), BF16),
                   jax.ShapeDtypeStruct((bn, s, v_w), BF16), jax.ShapeDtypeStruct((bn, s, v_w), F32),
                   jax.ShapeDtypeStruct((bn, s, ng), F32)],
        compiler_params=_params(("parallel", "parallel"), 2 * (tm * d * 4 + d * n * 2) + 8 * tm * n * 4),
        name="ml_in_proj",
    )(xcat, shl, scl, shc, scc, w, b, wg, bg)


def _log_sigmoid(x):
    return jnp.minimum(x, 0.0) - jnp.log1p(jnp.exp(-jnp.abs(x)))


def _mlstm_kernel(qf_ref, kf_ref, vf_ref, gf_ref, gtf_ref, qb_ref, kb_ref, vb_ref, gb_ref, gtb_ref,
                  hf_ref, hb_ref, ct_ref, n_ref, m_ref):
    @pl.when(pl.program_id(1) == 0)
    def _():
        ct_ref[...] = jnp.zeros_like(ct_ref)
        n_ref[...] = jnp.zeros_like(n_ref)
        m_ref[...] = jnp.zeros_like(m_ref)

    rows_per_step, ch = qf_ref.shape[0], qf_ref.shape[1]
    dk = qf_ref.shape[2] // ML_HEADS
    dv = vf_ref.shape[2] // ML_HEADS
    row = lax.broadcasted_iota(jnp.int32, (ch, ch), 0)
    col = lax.broadcasted_iota(jnp.int32, (ch, ch), 1)
    chains = []
    for bi in range(rows_per_step):
        for backward, q_ref, k_ref, v_ref, g_ref, gt_ref, h_ref in (
                (False, qf_ref, kf_ref, vf_ref, gf_ref, gtf_ref, hf_ref),
                (True, qb_ref, kb_ref, vb_ref, gb_ref, gtb_ref, hb_ref)):
            seen = (col >= row) if backward else (col <= row)
            g = g_ref[0, bi]
            gt = gt_ref[0, bi, 0]
            f_cols = _log_sigmoid(g[:, ML_HEADS:])
            f_rows = _log_sigmoid(gt[ML_HEADS:, :])
            seen_f = seen.astype(F32)
            b_cols = jnp.dot(seen_f, f_cols, precision=lax.Precision.HIGHEST, preferred_element_type=F32)
            b_rows = lax.dot_general(f_rows, seen_f, (((1,), (1,)), ((), ())), precision=lax.Precision.HIGHEST,
                                     preferred_element_type=F32)
            for head in range(ML_HEADS):
                chains.append(dict(
                    st=(2 * bi + int(backward)) * ML_HEADS + head, head=head, bi=bi, seen=seen, h_ref=h_ref,
                    i_col=g[:, head:head + 1], i_row=gt[head:head + 1, :], f_row=f_rows[head:head + 1, :],
                    b_col=b_cols[:, head:head + 1], b_row=b_rows[head:head + 1, :],
                    q=q_ref[bi, :, head * dk:(head + 1) * dk], k=k_ref[bi, :, head * dk:(head + 1) * dk],
                    v=v_ref[bi, :, head * dv:(head + 1) * dv]))
    for c in chains:
        c["m"] = m_ref[c["st"]]
        c["a_col"] = c["b_col"] + c["m"]
        c["dlog"] = jnp.where(c["seen"], c["b_col"] - c["b_row"] + c["i_row"], NEG_INF)
        c["qk"] = _dot_nt(c["q"], c["k"])
        c["ct"] = ct_ref[c["st"]]
        c["qc"] = _dot(c["q"], c["ct"].astype(BF16))
    for c in chains:
        c["mt"] = jnp.maximum(c["a_col"], jnp.max(c["dlog"], axis=1, keepdims=True))
        c["qn"] = jnp.sum(c["q"].astype(F32) * n_ref[c["st"]], axis=1, keepdims=True)
        c["bl"] = jnp.sum(c["f_row"], axis=1, keepdims=True)
        c["wl"] = c["bl"] - c["b_col"] + c["i_col"]
    for c in chains:
        c["smat"] = c["qk"] * jnp.exp(c["dlog"] - c["mt"])
        c["aw"] = jnp.exp(c["a_col"] - c["mt"])
        c["mn"] = jnp.maximum(c["bl"] + c["m"], jnp.max(c["wl"], axis=0, keepdims=True))
    for c in chains:
        c["sv"] = _dot(c["smat"].astype(BF16), c["v"])
        c["den"] = jnp.sum(c["smat"], axis=1, keepdims=True) + c["aw"] * c["qn"]
        wc = jnp.exp(c["wl"] - c["mn"])
        c["wc"] = wc
        c["kv"] = _dot_tn(c["k"], (wc * c["v"].astype(F32)).astype(BF16))
    for c in chains:
        head, st = c["head"], c["st"]
        num = c["sv"] + c["aw"] * c["qc"]
        c["h_ref"][c["bi"], :, head * dv:(head + 1) * dv] = (
            num / jnp.maximum(jnp.abs(c["den"]), jnp.exp(-c["mt"])))
        dc = jnp.exp(c["bl"] + c["m"] - c["mn"])
        ct_ref[st] = dc * c["ct"] + c["kv"]
        n_ref[st] = dc * n_ref[st] + jnp.sum(c["wc"] * c["k"].astype(F32), axis=0, keepdims=True)
        m_ref[st] = c["mn"]


def _mlstm(q, k, v, g2, gt2, n_ctx_chunks):
    bn, s, qk_w = q.shape
    v_w = v.shape[2]
    ch = ML_CHUNK
    nc = s // ch
    ncx = n_ctx_chunks

    def back(t):
        return jnp.where(t < ncx, ncx - 1 - t, nc + ncx - 1 - t)

    nb = math.gcd(bn, ML_BATCH_TILE)

    def specs(d, chunk):
        seq = lambda b, t: (b, chunk(t), 0)
        return [pl.BlockSpec((nb, ch, qk_w), seq), pl.BlockSpec((nb, ch, qk_w), seq),
                pl.BlockSpec((nb, ch, v_w), seq),
                pl.BlockSpec((1, nb, ch, 2 * ML_HEADS), lambda b, t: (d, b, chunk(t), 0)),
                pl.BlockSpec((1, nb, 1, 2 * ML_HEADS, ch), lambda b, t: (d, b, chunk(t), 0, 0))]

    fwd = lambda t: t
    h_shape = jax.ShapeDtypeStruct((bn, s, v_w), F32)
    states = nb * 2 * ML_HEADS
    return pl.pallas_call(
        _mlstm_kernel,
        grid=(bn // nb, nc),
        in_specs=specs(0, fwd) + specs(1, back),
        out_specs=[pl.BlockSpec((nb, ch, v_w), lambda b, t: (b, t, 0)),
                   pl.BlockSpec((nb, ch, v_w), lambda b, t: (b, back(t), 0))],
        out_shape=[h_shape, h_shape],
        scratch_shapes=[pltpu.VMEM((states, qk_w // ML_HEADS, v_w // ML_HEADS), F32),
                        pltpu.VMEM((states, 1, qk_w // ML_HEADS), F32),
                        pltpu.VMEM((states, 1, 1), F32)],
        compiler_params=_params(("parallel", "arbitrary"), 16 * 1024 * 1024),
        name="mlstm_scan",
    )(q, k, v, g2, gt2, q, k, v, g2, gt2)


def _sort_network(lo, hi):
    def merge(lo, hi, r):
        step = 2 * r
        if step < hi - lo:
            yield from merge(lo, hi, step)
            yield from merge(lo + r, hi, step)
            yield from ((i, i + r) for i in range(lo + r, hi - r, step))
        else:
            yield (lo, lo + r)

    if hi - lo >= 1:
        mid = lo + (hi - lo) // 2
        yield from _sort_network(lo, mid)
        yield from _sort_network(mid + 1, hi)
        yield from merge(lo, hi, 1)


def _bitonic_merge_network(n):
    stride = n // 2
    while stride >= 1:
        yield from ((i, i + stride) for i in range(n) if not i & stride)
        stride //= 2


def _compare_exchange(x, pairs):
    for i, j in pairs:
        x[i], x[j] = jnp.maximum(x[i], x[j]), jnp.minimum(x[i], x[j])


def _top_values(s, vals_ref, lanes):
    n = s.shape[0] // V7X_SUBLANES
    assert n == PEER_TOPK
    x = [s[v * V7X_SUBLANES:(v + 1) * V7X_SUBLANES] for v in range(n)]
    _compare_exchange(x, list(_sort_network(0, n - 1)))
    shift = V7X_SUBLANES // 2
    while shift >= 1:
        x = [jnp.maximum(x[i], pltpu.roll(x[n - 1 - i], shift, 0)) for i in range(n)]
        _compare_exchange(x, list(_bitonic_merge_network(n)))
        shift //= 2
    for i in range(n):
        vals_ref[i:i + 1, lanes] = x[i][0:1]


def _peer_sel_kernel(hp_ref, wqt_ref, k1_ref, k2_ref, r2_ref, e2_ref, kk_ref, cc_ref, v1_ref, v2_ref, cnt_ref):
    half = N_KEYS
    qt = _dot_nt(wqt_ref[...], hp_ref[...])
    for h in range(PEER_HEADS):
        q1 = qt[(2 * h) * half:(2 * h + 1) * half].astype(BF16)
        q2 = qt[(2 * h + 1) * half:(2 * h + 2) * half].astype(BF16)
        s1_all = _dot(k1_ref[h], q1)
        s2_all = _dot(k2_ref[h], q2)
        for lt in range(s1_all.shape[1] // V7X_LANES):
            lanes = slice(lt * V7X_LANES, (lt + 1) * V7X_LANES)
            s1 = s1_all[:, lanes]
            s2 = s2_all[:, lanes]
            _top_values(s1, v1_ref, lanes)
            _top_values(s2, v2_ref, lanes)
            v1 = v1_ref[:, lanes]
            v2 = v2_ref[:, lanes]
            cand = jnp.concatenate(
                [v1 + v2[0:1]] + [v1[0:8] + v2[j:j + 1] for j in range(1, 8)] + [v1[0:1] + v2[8:16]], axis=0)
            work = cand
            theta = None
            for _ in range(PEER_TOPK):
                theta = jnp.max(work, axis=0, keepdims=True)
                work = jnp.where(work == theta, NEG_INF, work)
            smax = v1[0:1] + v2[0:1]
            sel = cand >= theta
            z = jnp.sum(jnp.where(sel, jnp.exp(cand - smax), 0.0), axis=0, keepdims=True)
            picked = sel.astype(F32)
            cnt_ref[:, lanes] = picked[0:16]
            cnt_ref[0:8, lanes] += functools.reduce(jnp.add, [picked[8 + 8 * j:16 + 8 * j] for j in range(1, 8)])
            cnt_ref[0:1, lanes] += jnp.sum(picked[72:80], axis=0, keepdims=True)
            kk = jnp.zeros(s1.shape, F32)
            for i in range(PEER_TOPK):
                kk = jnp.where(s1 == v1_ref[i:i + 1, lanes], cnt_ref[i:i + 1, lanes], kk)
            r2 = jnp.full(s2.shape, float(PEER_TOPK), F32)
            for jr in reversed(range(PEER_TOPK)):
                r2 = jnp.where(s2 >= v2_ref[jr:jr + 1, lanes], float(jr), r2)
            r2 = r2.astype(BF16)
            e2 = jnp.exp(s2 - v2[0:1]).astype(BF16)
            for grp in range(N_KEYS // V7X_BF16_ROWS):
                r2_ref[0, h, grp, :, lanes] = r2[grp * V7X_BF16_ROWS:(grp + 1) * V7X_BF16_ROWS]
                e2_ref[0, h, grp, :, lanes] = e2[grp * V7X_BF16_ROWS:(grp + 1) * V7X_BF16_ROWS]
            kk_ref[0, h, :, lanes] = kk.astype(BF16)
            cc_ref[0, h, :, lanes] = (jnp.exp(s1 - v1[0:1]) / z).astype(BF16)


def _peer_select(hp, wqt, k1, k2, t):
    nt, d = hp.shape
    nb = nt // t
    sel_spec = pl.BlockSpec((1, PEER_HEADS, N_KEYS, t), lambda i: (i, 0, 0, 0))
    sel_shape = jax.ShapeDtypeStruct((nb, PEER_HEADS, N_KEYS, t), BF16)
    groups = N_KEYS // V7X_BF16_ROWS
    pk_spec = pl.BlockSpec((1, PEER_HEADS, groups, V7X_BF16_ROWS, t), lambda i: (i, 0, 0, 0, 0))
    pk_shape = jax.ShapeDtypeStruct((nb, PEER_HEADS, groups, V7X_BF16_ROWS, t), BF16)
    return pl.pallas_call(
        _peer_sel_kernel,
        grid=(nb,),
        in_specs=[pl.BlockSpec((t, d), lambda i: (i, 0)), pl.BlockSpec(wqt.shape, lambda i: (0, 0)),
                  pl.BlockSpec(k1.shape, lambda i: (0, 0, 0)), pl.BlockSpec(k2.shape, lambda i: (0, 0, 0))],
        out_specs=[pk_spec, pk_spec, sel_spec, sel_spec],
        out_shape=[pk_shape, pk_shape, sel_shape, sel_shape],
        scratch_shapes=[pltpu.VMEM((PEER_TOPK, t), F32), pltpu.VMEM((PEER_TOPK, t), F32),
                        pltpu.VMEM((PEER_TOPK, t), F32)],
        compiler_params=_params(("parallel",), 40 * 1024 * 1024),
        name="peer_select",
    )(hp, wqt, k1, k2)


def _gelu(x):
    return 0.5 * x * (1.0 + lax.erf(x * (2.0 ** -0.5)))


def _peer_dense_kernel(hp_ref, u_ref, vt_ref, r2_ref, e2_ref, kk_ref, cc_ref, x1_ref, g2_ref, lng_ref, lnb_ref,
                       x2_ref, acc_ref, w_ref, r2s_ref, e2s_ref, act_ref):
    j = pl.program_id(1)
    tn, t = u_ref.shape[0], hp_ref.shape[0]
    na = tn // N_KEYS

    @pl.when(j == 0)
    def _():
        acc_ref[...] = jnp.zeros_like(acc_ref)
        r2s_ref[...] = r2_ref[0]
        e2s_ref[...] = e2_ref[0]

    a_rows = pl.ds(pl.multiple_of(j * na, na), na)
    tile = (N_KEYS // V7X_BF16_ROWS, V7X_BF16_ROWS, V7X_LANES)
    for ag in range(0, na, PEER_A_GROUP):
        rows = slice(ag * N_KEYS, (ag + PEER_A_GROUP) * N_KEYS)
        act_ref[...] = _gelu(_dot_nt(u_ref[rows, :], hp_ref[...])).astype(BF16).reshape(act_ref.shape)

        for lt in range(t // V7X_LANES):
            lanes = slice(lt * V7X_LANES, (lt + 1) * V7X_LANES)
            gsum = [jnp.zeros(tile, BF16) for _ in range(PEER_A_GROUP)]
            for h in range(PEER_HEADS):
                r2 = r2s_ref[h, :, :, lanes]
                e2 = e2s_ref[h, :, :, lanes]
                kk = kk_ref[0, h, a_rows, lanes].astype(F32)
                cc = cc_ref[0, h, a_rows, lanes].astype(F32)
                for i in range(PEER_A_GROUP):
                    al = ag + i
                    kk_t = jnp.broadcast_to(kk[al:al + 1], tile[1:]).astype(BF16)[None]
                    cc_t = jnp.broadcast_to(cc[al:al + 1], tile[1:]).astype(BF16)[None]
                    gsum[i] = jnp.where(r2 < kk_t, gsum[i] + e2 * cc_t, gsum[i])
            for i in range(PEER_A_GROUP):
                w_ref[(ag + i) * tile[0]:(ag + i + 1) * tile[0], :, lanes] = (
                    gsum[i] * act_ref[i * tile[0]:(i + 1) * tile[0], :, lanes])
    acc_ref[...] += _dot(vt_ref[...], w_ref[...].reshape(tn, t))

    @pl.when(j == pl.num_programs(1) - 1)
    def _():
        y = ALPHA * x1_ref[...] + g2_ref[0] * acc_ref[...].T
        x2_ref[...] = _ln(y) * lng_ref[...] + lnb_ref[...]


def _peer_dense(hp, u, vt, layer, sel, x1, g2, blocks_per_row, lng, lnb, t, tn):
    nt, d = hp.shape
    ne = u.shape[1]
    sel_spec = pl.BlockSpec((1, PEER_HEADS, N_KEYS, t), lambda i, j: (i, 0, 0, 0))
    pk_spec = pl.BlockSpec((1, PEER_HEADS, N_KEYS // V7X_BF16_ROWS, V7X_BF16_ROWS, t), lambda i, j: (i, 0, 0, 0, 0))
    fix = lambda i, j: (0, 0)
    return pl.pallas_call(
        _peer_dense_kernel,
        grid=(nt // t, ne // tn),
        in_specs=[pl.BlockSpec((t, d), lambda i, j: (i, 0)), pl.BlockSpec((None, tn, d), lambda i, j: (layer, j, 0)),
                  pl.BlockSpec((None, d, tn), lambda i, j: (layer, 0, j)), pk_spec, pk_spec, sel_spec, sel_spec,
                  pl.BlockSpec((t, d), lambda i, j: (i, 0)),
                  pl.BlockSpec((1, 1, d), lambda i, j: (i // blocks_per_row, 0, 0)),
                  pl.BlockSpec((1, d), fix), pl.BlockSpec((1, d), fix)],
        out_specs=pl.BlockSpec((t, d), lambda i, j: (i, 0)),
        out_shape=jax.ShapeDtypeStruct((nt, d), F32),
        scratch_shapes=[pltpu.VMEM((d, t), F32), pltpu.VMEM((tn // V7X_BF16_ROWS, V7X_BF16_ROWS, t), BF16),
                        pltpu.VMEM(pk_spec.block_shape[1:], BF16), pltpu.VMEM(pk_spec.block_shape[1:], BF16),
                        pltpu.VMEM((PEER_A_GROUP * N_KEYS // V7X_BF16_ROWS, V7X_BF16_ROWS, t), BF16)],
        compiler_params=_params(("parallel", "arbitrary"), 48 * 1024 * 1024),
        name="peer_dense",
    )(hp, u, vt, *sel, x1, g2, lng, lnb)


def _peer_block(x1, hp, g2, wqt, k1, k2, u, vt, layer, lng, lnb):
    bn, l, d = x1.shape
    t = min(512, l)
    tn = 16 * N_KEYS
    hp2 = hp.reshape(bn * l, d)
    sel = _peer_select(hp2, wqt, k1, k2, t)
    x2 = _peer_dense(hp2, u, vt, layer, sel, x1.reshape(bn * l, d), g2, l // t, lng, lnb, t, tn)
    return x2.reshape(bn, l, d)


def _rope_tables(l, att_w):
    rows = l // GRID_W
    row = jnp.repeat(jnp.arange(rows, dtype=F32), GRID_W)
    col = jnp.tile(jnp.arange(GRID_W, dtype=F32), rows)
    f = HEAD_DIM // 4
    inv_freq = ROPE_BASE ** (-jnp.arange(f, dtype=F32) / f)
    ar = row[:, None] * inv_freq[None, :]
    ac = col[:, None] * inv_freq[None, :]
    cos = jnp.concatenate([jnp.cos(ar), jnp.cos(ar), jnp.cos(ac), jnp.cos(ac)], axis=-1)
    sin = jnp.concatenate([-jnp.sin(ar), jnp.sin(ar), -jnp.sin(ac), jnp.sin(ac)], axis=-1)
    reps = att_w // HEAD_DIM
    return jnp.tile(cos, (1, reps)), jnp.tile(sin, (1, reps))


def _block_diag_ones(width, group):
    idx = jnp.arange(width) // group
    return (idx[:, None] == idx[None, :]).astype(BF16)


def kernel(x, c, ctx, c_ctx, w_mod, b_mod, ln1_g, ln1_b, ln2_g, ln2_b, ab_w_in, ab_b_in, ab_conv_w, ab_conv_b,
           ab_conv_ln_g, ab_conv_ln_b, ab_q_norm_g, ab_k_norm_g, ab_w_out, ab_b_out, ml_w_in, ml_b_in, ml_norm_g,
           ml_w_out, ml_b_out, peer_w_q, peer_k1, peer_k2, peer_u, peer_v):
    bsz, l, d = x.shape
    lc = ctx.shape[1]
    assert l % GRID_W == 0 and lc % ML_CHUNK == 0 and l % lc == 0

    rows = -(-(bsz + 1) // 8) * 8
    c_rows = jnp.zeros((rows, d), F32).at[:bsz].set(c).at[bsz].set(c_ctx)
    mod = _modulation(c_rows, w_mod, b_mod)

    def mod_rows(i):
        m = mod[i].reshape(rows, N_MOD, d)
        lat = [m[:bsz, k][:, None, :] for k in range(N_MOD)]
        cx = [m[bsz:bsz + 1, k][:, None, :] for k in range(N_MOD)]
        return lat, cx

    row2 = lambda v: v.reshape(1, -1)

    conv_ch = ab_conv_w.shape[2]
    kv_w = ATT_KV_HEADS * HEAD_DIM
    att_w = ATT_HEADS * HEAD_DIM
    (sh1l, sc1l, g1l, sh2l, sc2l, g2l), (sh1c, sc1c, g1c, sh2c, sc2c, g2c) = mod_rows(0)
    bcast = lambda v: jnp.broadcast_to(v, (bsz, 1, d))
    w_in = ab_w_in[0].astype(BF16)
    b_in = row2(ab_b_in[0])
    qg = row2(jnp.tile(ab_q_norm_g[0], ATT_HEADS))
    kg = row2(jnp.tile(ab_k_norm_g[0], ATT_KV_HEADS))
    bd = _block_diag_ones(2 * V7X_LANES, HEAD_DIM)
    cos, sin = _rope_tables(l, att_w)
    ul, ql, kl, vl = _ab_in(x, sh1l, sc1l, w_in, b_in, qg, kg, bd, cos, sin, True, conv_ch, att_w, kv_w)
    uc, qc, kc, vc = _ab_in(ctx, bcast(sh1c), bcast(sc1c), w_in, b_in, qg, kg, bd, cos[:lc], sin[:lc], False,
                            conv_ch, att_w, kv_w)
    conv_args = (ab_conv_w[0], row2(ab_conv_b[0]), row2(ab_conv_ln_g[0]), row2(ab_conv_ln_b[0]))
    conv_l = _conv_group(ul, *conv_args)
    conv_c = _conv_group(uc, *conv_args)
    att_l = _attention(ql, jnp.concatenate([kc, kl], axis=1), jnp.concatenate([vc, vl], axis=1))
    att_c = _attention(qc, kc, vc)
    w_out = ab_w_out[0].astype(BF16)
    out_args = (w_out[:conv_ch], w_out[conv_ch:], row2(ab_b_out[0]))
    ln1 = (row2(ln1_g[0]), row2(ln1_b[0]))
    x1, hpl = _ab_out(conv_l, att_l, *out_args, x, g1l, *ln1, sh2l, sc2l)
    c1, hpc = _ab_out(conv_c, att_c, *out_args, ctx, bcast(g1c), *ln1, bcast(sh2c), bcast(sc2c))

    assert peer_k1.shape[3] == N_KEYS
    u_all = peer_u.astype(BF16)
    vt_all = jnp.swapaxes(peer_v, 1, 2).astype(BF16)

    def peer_weights(i):
        wqt = peer_w_q[i].T.astype(BF16)
        return (wqt, peer_k1[i].astype(BF16), peer_k2[i].astype(BF16), u_all, vt_all, i,
                row2(ln2_g[i]), row2(ln2_b[i]))

    pw = peer_weights(0)
    x = _peer_block(x1, hpl, g2l, *pw)
    ctx = _peer_block(c1.reshape(1, bsz * lc, d), hpc.reshape(1, bsz * lc, d), g2c, *pw).reshape(bsz, lc, d)

    (sh1l, sc1l, g1l, sh2l, sc2l, g2l), (sh1c, sc1c, _, _, _, _) = mod_rows(1)
    qk_w = ML_HEADS * (d // 8)
    v_w = ML_HEADS * (d // 4)
    n_main = 2 * qk_w + 2 * v_w
    w_in = ml_w_in[0]
    wg = jnp.zeros((d, V7X_LANES), F32).at[:, :4 * ML_HEADS].set(w_in[:, n_main:]).astype(BF16)
    bg = jnp.zeros((1, V7X_LANES), F32).at[:, :4 * ML_HEADS].set(ml_b_in[0][n_main:])
    xcat = jnp.concatenate([ctx, x], axis=1)
    q, k, v, o, g = _ml_in(xcat, lc, sh1l, sc1l, sh1c, sc1c, w_in[:, :n_main].astype(BF16),
                           row2(ml_b_in[0][:n_main]), wg, bg, qk_w, v_w)
    s = lc + l
    g2 = g.reshape(bsz, s, 2, 2 * ML_HEADS).transpose(2, 0, 1, 3)
    gt2 = g2.reshape(2, bsz, s // ML_CHUNK, ML_CHUNK, 2 * ML_HEADS).transpose(0, 1, 2, 4, 3)
    hf, hb = _mlstm(q, k, v, g2, gt2, lc // ML_CHUNK)
    x1, hpl = _ml_out(hf, hb, o, lc, row2(ml_norm_g[0]), ml_w_out[0].astype(BF16), row2(ml_b_out[0]), x, g1l,
                      row2(ln1_g[1]), row2(ln1_b[1]), sh2l, sc2l)
    return _peer_block(x1, hpl, g2l, *peer_weights(1))
```

```python
import functools
import math

import jax
import jax.numpy as jnp
from jax import lax
from jax.experimental import pallas as pl
from jax.experimental.pallas import tpu as pltpu

F32 = jnp.float32
BF16 = jnp.bfloat16

DEPTH = 2
EPS = 1e-6
ALPHA = (2 * DEPTH) ** 0.25
N_MOD = 6
GRID_W = 64

CONV_WIDTH = 31
ATT_HEADS = 8
ATT_KV_HEADS = 2
HEAD_DIM = 64
ROPE_BASE = 10000.0

ML_HEADS = 4
ML_CHUNK = 64
ML_BATCH_TILE = 4

PEER_HEADS = 8
N_KEYS = 128
PEER_TOPK = 16
PEER_A_GROUP = 4

V7X_VMEM_BYTES = 64 * 1024 * 1024
V7X_LANES = 128
V7X_SUBLANES = 8
V7X_BF16_ROWS = 16
NEG_INF = float("-inf")


def _params(semantics, vmem_bytes):
    return pltpu.CompilerParams(dimension_semantics=semantics,
                                vmem_limit_bytes=min(int(vmem_bytes), V7X_VMEM_BYTES - 8 * 1024 * 1024))


def _ln(x):
    mu = jnp.mean(x, axis=-1, keepdims=True)
    xc = x - mu
    var = jnp.mean(xc * xc, axis=-1, keepdims=True)
    return xc * lax.rsqrt(var + EPS)


def _dot(a, b):
    return jnp.dot(a, b, preferred_element_type=F32)


def _dot_nt(a, b):
    return lax.dot_general(a, b, (((1,), (1,)), ((), ())), preferred_element_type=F32)


def _dot_tn(a, b):
    return lax.dot_general(a, b, (((0,), (0,)), ((), ())), preferred_element_type=F32)


def _mod_kernel(c_ref, w_ref, b_ref, o_ref):
    c = c_ref[...]
    s = (c * jax.nn.sigmoid(c)).astype(BF16)
    o_ref[0] = _dot(s, w_ref[0].astype(BF16)) + b_ref[0]


def _modulation(c_rows, w_mod, b_mod):
    depth, d, n = w_mod.shape
    rows = c_rows.shape[0]
    tn = 1536
    return pl.pallas_call(
        _mod_kernel,
        grid=(depth, n // tn),
        in_specs=[pl.BlockSpec((rows, d), lambda i, j: (0, 0)),
                  pl.BlockSpec((1, d, tn), lambda i, j: (i, 0, j)),
                  pl.BlockSpec((1, 1, tn), lambda i, j: (i, 0, j))],
        out_specs=pl.BlockSpec((1, rows, tn), lambda i, j: (i, 0, j)),
        out_shape=jax.ShapeDtypeStruct((depth, rows, n), F32),
        compiler_params=_params(("parallel", "parallel"), 4 * d * tn * 4),
        name="modulation",
    )(c_rows, w_mod, b_mod.reshape(depth, 1, n))


def _rope(x, cos, sin_signed):
    w = x.shape[-1]
    lane = lax.broadcasted_iota(jnp.int32, x.shape, 1)
    first = (lane % 32) < 16
    partner = jnp.where(first, pltpu.roll(x, w - 16, 1), pltpu.roll(x, 16, 1))
    return x * cos + partner * sin_signed


def _group_rms(x, bd, g):
    w = bd.shape[0]
    parts = []
    for s in range(0, x.shape[-1], w):
        xs = x[:, s:s + w]
        ms = _dot((xs * xs).astype(BF16), bd) * (1.0 / HEAD_DIM)
        parts.append(xs * lax.rsqrt(ms + EPS))
    y = parts[0] if len(parts) == 1 else jnp.concatenate(parts, axis=-1)
    return y * g


def _ab_in_kernel(use_rope, conv_ch, att_w, kv_w, x_ref, sh_ref, sc_ref, w_ref, b_ref, qg_ref, kg_ref, bd_ref,
                  cos_ref, sin_ref, u_ref, q_ref, k_ref, v_ref):
    h = _ln(x_ref[0]) * (1.0 + sc_ref[0]) + sh_ref[0]
    p = _dot(h.astype(BF16), w_ref[...]) + b_ref[...]
    c1, c2, c3, c4 = conv_ch, 2 * conv_ch, 2 * conv_ch + att_w, 2 * conv_ch + att_w + kv_w
    u_ref[0] = p[:, :c1] * jax.nn.sigmoid(p[:, c1:c2])
    q = _group_rms(p[:, c2:c3], bd_ref[...], qg_ref[...])
    k = _group_rms(p[:, c3:c4], bd_ref[:kv_w, :kv_w], kg_ref[...])
    if use_rope:
        q = _rope(q, cos_ref[...], sin_ref[...])
        k = _rope(k, cos_ref[:, :kv_w], sin_ref[:, :kv_w])
    q_ref[0] = (q * (HEAD_DIM ** -0.5)).astype(BF16)
    k_ref[0] = k.astype(BF16)
    v_ref[0] = p[:, c4:].astype(BF16)


def _ab_in(x, sh, sc, w, b, qg, kg, bd, cos, sin, use_rope, conv_ch, att_w, kv_w):
    bn, l, d = x.shape
    n = w.shape[1]
    tm = min(512, l)
    kern = functools.partial(_ab_in_kernel, use_rope, conv_ch, att_w, kv_w)
    row = lambda i, j: (i, j, 0)
    mod = lambda i, j: (i, 0, 0)
    fix = lambda i, j: (0, 0)
    return pl.pallas_call(
        kern,
        grid=(bn, l // tm),
        in_specs=[pl.BlockSpec((1, tm, d), row), pl.BlockSpec((1, 1, d), mod), pl.BlockSpec((1, 1, d), mod),
                  pl.BlockSpec((d, n), fix), pl.BlockSpec((1, n), fix),
                  pl.BlockSpec((1, att_w), fix), pl.BlockSpec((1, kv_w), fix), pl.BlockSpec(bd.shape, fix),
                  pl.BlockSpec((tm, att_w), lambda i, j: (j, 0)), pl.BlockSpec((tm, att_w), lambda i, j: (j, 0))],
        out_specs=[pl.BlockSpec((1, tm, conv_ch), row), pl.BlockSpec((1, tm, att_w), row),
                   pl.BlockSpec((1, tm, kv_w), row), pl.BlockSpec((1, tm, kv_w), row)],
        out_shape=[jax.ShapeDtypeStruct((bn, l, conv_ch), F32), jax.ShapeDtypeStruct((bn, l, att_w), BF16),
                   jax.ShapeDtypeStruct((bn, l, kv_w), BF16), jax.ShapeDtypeStruct((bn, l, kv_w), BF16)],
        compiler_params=_params(("parallel", "parallel"), 2 * (tm * d * 4 + d * n * 2) + 8 * tm * n * 4),
        name="ab_in_proj",
    )(x, sh, sc, w, b, qg, kg, bd, cos, sin)


def _attn_kernel(q_ref, k_ref, v_ref, o_ref):
    group = ATT_HEADS // ATT_KV_HEADS
    for kh in range(ATT_KV_HEADS):
        kk = k_ref[0, :, kh * HEAD_DIM:(kh + 1) * HEAD_DIM]
        vv = v_ref[0, :, kh * HEAD_DIM:(kh + 1) * HEAD_DIM]
        for g in range(group):
            lo = (kh * group + g) * HEAD_DIM
            s = _dot_nt(q_ref[0, :, lo:lo + HEAD_DIM], kk)
            p = jnp.exp(s - jnp.max(s, axis=-1, keepdims=True))
            denom = jnp.sum(p, axis=-1, keepdims=True)
            o_ref[0, :, lo:lo + HEAD_DIM] = _dot(p.astype(BF16), vv) / denom


def _attention(q, k, v):
    bn, lq, w = q.shape
    s, kvw = k.shape[1], k.shape[2]
    tq = min(256, lq)
    return pl.pallas_call(
        _attn_kernel,
        grid=(bn, lq // tq),
        in_specs=[pl.BlockSpec((1, tq, w), lambda i, j: (i, j, 0)),
                  pl.BlockSpec((1, s, kvw), lambda i, j: (i, 0, 0)),
                  pl.BlockSpec((1, s, kvw), lambda i, j: (i, 0, 0))],
        out_specs=pl.BlockSpec((1, tq, w), lambda i, j: (i, j, 0)),
        out_shape=jax.ShapeDtypeStruct((bn, lq, w), F32),
        compiler_params=_params(("parallel", "parallel"), 8 * tq * s * 4 + 8 * s * kvw * 2),
        name="gqa_attention",
    )(q, k, v)


CONV_ROWS = 32
CONV_HALO = 16


def _conv_kernel(u_ref, w_ref, cb_ref, g_ref, b_ref, o_ref, pad_ref, shift_ref):
    l, ch = u_ref.shape[1], u_ref.shape[2]
    zeros = jnp.zeros((CONV_HALO, ch), F32)
    pad_ref[0:CONV_HALO, :] = zeros
    pad_ref[CONV_HALO + l:2 * CONV_HALO + l, :] = zeros
    pad_ref[CONV_HALO:CONV_HALO + l, :] = u_ref[0]
    first = CONV_HALO - CONV_WIDTH // 2

    def tile(r, carry):
        base = pl.multiple_of(r * CONV_ROWS, CONV_ROWS)
        acc = jnp.zeros((CONV_ROWS, ch), F32)
        win = pad_ref[pl.ds(base, CONV_ROWS + 2 * CONV_HALO), :]
        for res in range(V7X_SUBLANES):
            taps = [j for j in range(CONV_WIDTH) if (first + j) % V7X_SUBLANES == res]
            span = max(first + j - res for j in taps) + CONV_ROWS
            shift_ref[0:span, :] = win[res:res + span, :]
            for j in taps:
                off = first + j - res
                acc = acc + w_ref[j:j + 1, :] * shift_ref[off:off + CONV_ROWS, :]
        y = _ln(acc + cb_ref[...]) * g_ref[...] + b_ref[...]
        o_ref[0, pl.ds(base, CONV_ROWS), :] = y * jax.nn.sigmoid(y)
        return carry

    lax.fori_loop(0, l // CONV_ROWS, tile, 0)


def _conv_group(u, w, cb, g, b):
    bn, l, ch = u.shape
    fix = lambda i: (0, 0)
    return pl.pallas_call(
        _conv_kernel,
        grid=(bn,),
        in_specs=[pl.BlockSpec((1, l, ch), lambda i: (i, 0, 0)), pl.BlockSpec((CONV_WIDTH, ch), fix),
                  pl.BlockSpec((1, ch), fix), pl.BlockSpec((1, ch), fix), pl.BlockSpec((1, ch), fix)],
        out_specs=pl.BlockSpec((1, l, ch), lambda i: (i, 0, 0)),
        out_shape=jax.ShapeDtypeStruct((bn, l, ch), F32),
        scratch_shapes=[pltpu.VMEM((l + 2 * CONV_HALO, ch), F32), pltpu.VMEM((CONV_ROWS + 2 * CONV_HALO, ch), F32)],
        compiler_params=_params(("parallel",), 6 * l * ch * 4),
        name="conv_group",
    )(u, w, cb, g, b)


def _resid_epilogue(y, x_ref, g1_ref, lng_ref, lnb_ref, sh2_ref, sc2_ref, x1_ref, hp_ref):
    x1 = _ln(ALPHA * x_ref[0] + g1_ref[0] * y) * lng_ref[...] + lnb_ref[...]
    x1_ref[0] = x1
    hp_ref[0] = (_ln(x1) * (1.0 + sc2_ref[0]) + sh2_ref[0]).astype(BF16)


def _ab_out_kernel(a_ref, t_ref, w1_ref, w2_ref, b_ref, x_ref, g1_ref, lng_ref, lnb_ref, sh2_ref, sc2_ref,
                   x1_ref, hp_ref):
    y = _dot(a_ref[0].astype(BF16), w1_ref[...]) + _dot(t_ref[0].astype(BF16), w2_ref[...]) + b_ref[...]
    _resid_epilogue(y, x_ref, g1_ref, lng_ref, lnb_ref, sh2_ref, sc2_ref, x1_ref, hp_ref)


def _ab_out(conv, att, w1, w2, b, x, g1, lng, lnb, sh2, sc2):
    bn, l, d = x.shape
    tm = min(512, l)
    row = lambda i, j: (i, j, 0)
    mod = lambda i, j: (i, 0, 0)
    fix = lambda i, j: (0, 0)
    k1, k2 = conv.shape[2], att.shape[2]
    return pl.pallas_call(
        _ab_out_kernel,
        grid=(bn, l // tm),
        in_specs=[pl.BlockSpec((1, tm, k1), row), pl.BlockSpec((1, tm, k2), row),
                  pl.BlockSpec((k1, d), fix), pl.BlockSpec((k2, d), fix), pl.BlockSpec((1, d), fix),
                  pl.BlockSpec((1, tm, d), row), pl.BlockSpec((1, 1, d), mod),
                  pl.BlockSpec((1, d), fix), pl.BlockSpec((1, d), fix),
                  pl.BlockSpec((1, 1, d), mod), pl.BlockSpec((1, 1, d), mod)],
        out_specs=[pl.BlockSpec((1, tm, d), row), pl.BlockSpec((1, tm, d), row)],
        out_shape=[jax.ShapeDtypeStruct((bn, l, d), F32), jax.ShapeDtypeStruct((bn, l, d), BF16)],
        compiler_params=_params(("parallel", "parallel"), 16 * tm * d * 4),
        name="ab_out_proj",
    )(conv, att, w1, w2, b, x, g1, lng, lnb, sh2, sc2)


def _ml_out_kernel(hf_ref, hb_ref, o_ref, ng_ref, w_ref, b_ref, x_ref, g1_ref, lng_ref, lnb_ref, sh2_ref, sc2_ref,
                   x1_ref, hp_ref):
    dv = hf_ref.shape[2] // ML_HEADS
    hs = hf_ref[0] + hb_ref[0]
    parts = [_ln(hs[:, h * dv:(h + 1) * dv]) for h in range(ML_HEADS)]
    hn = jnp.concatenate(parts, axis=-1) * ng_ref[...]
    y = _dot((hn * jax.nn.sigmoid(o_ref[0])).astype(BF16), w_ref[...]) + b_ref[...]
    _resid_epilogue(y, x_ref, g1_ref, lng_ref, lnb_ref, sh2_ref, sc2_ref, x1_ref, hp_ref)


def _ml_out(hf, hb, o, seq_off, ng, w, b, x, g1, lng, lnb, sh2, sc2):
    bn, l, d = x.shape
    vw = o.shape[2]
    tm = math.gcd(256, math.gcd(l, seq_off))
    off = seq_off // tm
    row = lambda i, j: (i, j, 0)
    mod = lambda i, j: (i, 0, 0)
    fix = lambda i, j: (0, 0)
    return pl.pallas_call(
        _ml_out_kernel,
        grid=(bn, l // tm),
        in_specs=[pl.BlockSpec((1, tm, vw), lambda i, j: (i, j + off, 0)),
                  pl.BlockSpec((1, tm, vw), lambda i, j: (i, j + off, 0)),
                  pl.BlockSpec((1, tm, vw), lambda i, j: (i, j + off, 0)),
                  pl.BlockSpec((1, vw), fix), pl.BlockSpec((vw, d), fix), pl.BlockSpec((1, d), fix),
                  pl.BlockSpec((1, tm, d), row), pl.BlockSpec((1, 1, d), mod),
                  pl.BlockSpec((1, d), fix), pl.BlockSpec((1, d), fix),
                  pl.BlockSpec((1, 1, d), mod), pl.BlockSpec((1, 1, d), mod)],
        out_specs=[pl.BlockSpec((1, tm, d), row), pl.BlockSpec((1, tm, d), row)],
        out_shape=[jax.ShapeDtypeStruct((bn, l, d), F32), jax.ShapeDtypeStruct((bn, l, d), BF16)],
        compiler_params=_params(("parallel", "parallel"), 24 * tm * d * 4),
        name="ml_out_proj",
    )(hf, hb, o, ng, w, b, x, g1, lng, lnb, sh2, sc2)


def _ml_in_kernel(qk_w, v_w, x_ref, shl_ref, scl_ref, shc_ref, scc_ref, w_ref, b_ref, wg_ref, bg_ref,
                  q_ref, k_ref, v_ref, o_ref, g_ref):
    is_ctx = pl.program_id(1) == 0
    sh = jnp.where(is_ctx, shc_ref[0], shl_ref[0])
    sc = jnp.where(is_ctx, scc_ref[0], scl_ref[0])
    h = (_ln(x_ref[0]) * (1.0 + sc) + sh).astype(BF16)
    p = _dot(h, w_ref[...]) + b_ref[...]
    dqk = qk_w // ML_HEADS
    q_ref[0] = p[:, :qk_w].astype(BF16)
    k_ref[0] = (p[:, qk_w:2 * qk_w] * (dqk ** -0.5)).astype(BF16)
    v_ref[0] = p[:, 2 * qk_w:2 * qk_w + v_w].astype(BF16)
    o_ref[0] = p[:, 2 * qk_w + v_w:]
    g_ref[0] = (_dot(h, wg_ref[...]) + bg_ref[...])[:, :g_ref.shape[2]]


def _ml_in(xcat, lc, shl, scl, shc, scc, w, b, wg, bg, qk_w, v_w):
    bn, s, d = xcat.shape
    n = w.shape[1]
    ng = 4 * ML_HEADS
    tm = lc
    row = lambda i, j: (i, j, 0)
    mod = lambda i, j: (i, 0, 0)
    one = lambda i, j: (0, 0, 0)
    fix = lambda i, j: (0, 0)
    kern = functools.partial(_ml_in_kernel, qk_w, v_w)
    return pl.pallas_call(
        kern,
        grid=(bn, s // tm),
        in_specs=[pl.BlockSpec((1, tm, d), row), pl.BlockSpec((1, 1, d), mod), pl.BlockSpec((1, 1, d), mod),
                  pl.BlockSpec((1, 1, d), one), pl.BlockSpec((1, 1, d), one),
                  pl.BlockSpec((d, n), fix), pl.BlockSpec((1, n), fix),
                  pl.BlockSpec(wg.shape, fix), pl.BlockSpec(bg.shape, fix)],
        out_specs=[pl.BlockSpec((1, tm, qk_w), row), pl.BlockSpec((1, tm, qk_w), row),
                   pl.BlockSpec((1, tm, v_w), row), pl.BlockSpec((1, tm, v_w), row),
                   pl.BlockSpec((1, tm, ng), row)],
        out_shape=[jax.ShapeDtypeStruct((bn, s, qk_w), BF16), jax.ShapeDtypeStruct((bn, s, qk_w), BF16),
                   jax.ShapeDtypeStruct((bn, s, v_w), BF16), jax.ShapeDtypeStruct((bn, s, v_w), F32),
                   jax.ShapeDtypeStruct((bn, s, ng), F32)],
        compiler_params=_params(("parallel", "parallel"), 2 * (tm * d * 4 + d * n * 2) + 8 * tm * n * 4),
        name="ml_in_proj",
    )(xcat, shl, scl, shc, scc, w, b, wg, bg)


def _log_sigmoid(x):
    return jnp.minimum(x, 0.0) - jnp.log1p(jnp.exp(-jnp.abs(x)))


def _mlstm_kernel(qf_ref, kf_ref, vf_ref, gf_ref, gtf_ref, qb_ref, kb_ref, vb_ref, gb_ref, gtb_ref,
                  hf_ref, hb_ref, ct_ref, n_ref, m_ref):
    @pl.when(pl.program_id(1) == 0)
    def _():
        ct_ref[...] = jnp.zeros_like(ct_ref)
        n_ref[...] = jnp.zeros_like(n_ref)
        m_ref[...] = jnp.zeros_like(m_ref)

    rows_per_step, ch = qf_ref.shape[0], qf_ref.shape[1]
    dk = qf_ref.shape[2] // ML_HEADS
    dv = vf_ref.shape[2] // ML_HEADS
    row = lax.broadcasted_iota(jnp.int32, (ch, ch), 0)
    col = lax.broadcasted_iota(jnp.int32, (ch, ch), 1)
    chains = []
    for bi in range(rows_per_step):
        for backward, q_ref, k_ref, v_ref, g_ref, gt_ref, h_ref in (
                (False, qf_ref, kf_ref, vf_ref, gf_ref, gtf_ref, hf_ref),
                (True, qb_ref, kb_ref, vb_ref, gb_ref, gtb_ref, hb_ref)):
            seen = (col >= row) if backward else (col <= row)
            g = g_ref[0, bi]
            gt = gt_ref[0, bi, 0]
            f_cols = _log_sigmoid(g[:, ML_HEADS:])
            f_rows = _log_sigmoid(gt[ML_HEADS:, :])
            seen_f = seen.astype(F32)
            b_cols = jnp.dot(seen_f, f_cols, precision=lax.Precision.HIGHEST, preferred_element_type=F32)
            b_rows = lax.dot_general(f_rows, seen_f, (((1,), (1,)), ((), ())), precision=lax.Precision.HIGHEST,
                                     preferred_element_type=F32)
            for head in range(ML_HEADS):
                chains.append(dict(
                    st=(2 * bi + int(backward)) * ML_HEADS + head, head=head, bi=bi, seen=seen, h_ref=h_ref,
                    i_col=g[:, head:head + 1], i_row=gt[head:head + 1, :], f_row=f_rows[head:head + 1, :],
                    b_col=b_cols[:, head:head + 1], b_row=b_rows[head:head + 1, :],
                    q=q_ref[bi, :, head * dk:(head + 1) * dk], k=k_ref[bi, :, head * dk:(head + 1) * dk],
                    v=v_ref[bi, :, head * dv:(head + 1) * dv]))
    for c in chains:
        c["m"] = m_ref[c["st"]]
        c["a_col"] = c["b_col"] + c["m"]
        c["dlog"] = jnp.where(c["seen"], c["b_col"] - c["b_row"] + c["i_row"], NEG_INF)
        c["qk"] = _dot_nt(c["q"], c["k"])
        c["ct"] = ct_ref[c["st"]]
        c["qc"] = _dot(c["q"], c["ct"].astype(BF16))
    for c in chains:
        c["mt"] = jnp.maximum(c["a_col"], jnp.max(c["dlog"], axis=1, keepdims=True))
        c["qn"] = jnp.sum(c["q"].astype(F32) * n_ref[c["st"]], axis=1, keepdims=True)
        c["bl"] = jnp.sum(c["f_row"], axis=1, keepdims=True)
        c["wl"] = c["bl"] - c["b_col"] + c["i_col"]
    for c in chains:
        c["smat"] = c["qk"] * jnp.exp(c["dlog"] - c["mt"])
        c["aw"] = jnp.exp(c["a_col"] - c["mt"])
        c["mn"] = jnp.maximum(c["bl"] + c["m"], jnp.max(c["wl"], axis=0, keepdims=True))
    for c in chains:
        c["sv"] = _dot(c["smat"].astype(BF16), c["v"])
        c["den"] = jnp.sum(c["smat"], axis=1, keepdims=True) + c["aw"] * c["qn"]
        wc = jnp.exp(c["wl"] - c["mn"])
        c["wc"] = wc
        c["kv"] = _dot_tn(c["k"], (wc * c["v"].astype(F32)).astype(BF16))
    for c in chains:
        head, st = c["head"], c["st"]
        num = c["sv"] + c["aw"] * c["qc"]
        c["h_ref"][c["bi"], :, head * dv:(head + 1) * dv] = (
            num / jnp.maximum(jnp.abs(c["den"]), jnp.exp(-c["mt"])))
        dc = jnp.exp(c["bl"] + c["m"] - c["mn"])
        ct_ref[st] = dc * c["ct"] + c["kv"]
        n_ref[st] = dc * n_ref[st] + jnp.sum(c["wc"] * c["k"].astype(F32), axis=0, keepdims=True)
        m_ref[st] = c["mn"]


def _mlstm(q, k, v, g2, gt2, n_ctx_chunks):
    bn, s, qk_w = q.shape
    v_w = v.shape[2]
    ch = ML_CHUNK
    nc = s // ch
    ncx = n_ctx_chunks

    def back(t):
        return jnp.where(t < ncx, ncx - 1 - t, nc + ncx - 1 - t)

    nb = math.gcd(bn, ML_BATCH_TILE)

    def specs(d, chunk):
        seq = lambda b, t: (b, chunk(t), 0)
        return [pl.BlockSpec((nb, ch, qk_w), seq), pl.BlockSpec((nb, ch, qk_w), seq),
                pl.BlockSpec((nb, ch, v_w), seq),
                pl.BlockSpec((1, nb, ch, 2 * ML_HEADS), lambda b, t: (d, b, chunk(t), 0)),
                pl.BlockSpec((1, nb, 1, 2 * ML_HEADS, ch), lambda b, t: (d, b, chunk(t), 0, 0))]

    fwd = lambda t: t
    h_shape = jax.ShapeDtypeStruct((bn, s, v_w), F32)
    states = nb * 2 * ML_HEADS
    return pl.pallas_call(
        _mlstm_kernel,
        grid=(bn // nb, nc),
        in_specs=specs(0, fwd) + specs(1, back),
        out_specs=[pl.BlockSpec((nb, ch, v_w), lambda b, t: (b, t, 0)),
                   pl.BlockSpec((nb, ch, v_w), lambda b, t: (b, back(t), 0))],
        out_shape=[h_shape, h_shape],
        scratch_shapes=[pltpu.VMEM((states, qk_w // ML_HEADS, v_w // ML_HEADS), F32),
                        pltpu.VMEM((states, 1, qk_w // ML_HEADS), F32),
                        pltpu.VMEM((states, 1, 1), F32)],
        compiler_params=_params(("parallel", "arbitrary"), 32 * 1024 * 1024),
        name="mlstm_scan",
    )(q, k, v, g2, gt2, q, k, v, g2, gt2)


def _sort_network(lo, hi):
    def merge(lo, hi, r):
        step = 2 * r
        if step < hi - lo:
            yield from merge(lo, hi, step)
            yield from merge(lo + r, hi, step)
            yield from ((i, i + r) for i in range(lo + r, hi - r, step))
        else:
            yield (lo, lo + r)

    if hi - lo >= 1:
        mid = lo + (hi - lo) // 2
        yield from _sort_network(lo, mid)
        yield from _sort_network(mid + 1, hi)
        yield from merge(lo, hi, 1)


def _bitonic_merge_network(n):
    stride = n // 2
    while stride >= 1:
        yield from ((i, i + stride) for i in range(n) if not i & stride)
        stride //= 2


def _compare_exchange(x, pairs):
    for i, j in pairs:
        x[i], x[j] = jnp.maximum(x[i], x[j]), jnp.minimum(x[i], x[j])


def _top_values(s, vals_ref, lanes):
    n = s.shape[0] // V7X_SUBLANES
    assert n == PEER_TOPK
    x = [s[v * V7X_SUBLANES:(v + 1) * V7X_SUBLANES] for v in range(n)]
    _compare_exchange(x, list(_sort_network(0, n - 1)))
    shift = V7X_SUBLANES // 2
    while shift >= 1:
        x = [jnp.maximum(x[i], pltpu.roll(x[n - 1 - i], shift, 0)) for i in range(n)]
        _compare_exchange(x, list(_bitonic_merge_network(n)))
        shift //= 2
    for i in range(n):
        vals_ref[i:i + 1, lanes] = x[i][0:1]


def _peer_sel_kernel(hp_ref, wqt_ref, k1_ref, k2_ref, r2_ref, e2_ref, kk_ref, cc_ref, v1_ref, v2_ref, cnt_ref):
    half = N_KEYS
    qt = _dot_nt(wqt_ref[...], hp_ref[...])
    for h in range(PEER_HEADS):
        q1 = qt[(2 * h) * half:(2 * h + 1) * half].astype(BF16)
        q2 = qt[(2 * h + 1) * half:(2 * h + 2) * half].astype(BF16)
        s1_all = _dot(k1_ref[h], q1)
        s2_all = _dot(k2_ref[h], q2)
        for lt in range(s1_all.shape[1] // V7X_LANES):
            lanes = slice(lt * V7X_LANES, (lt + 1) * V7X_LANES)
            s1 = s1_all[:, lanes]
            s2 = s2_all[:, lanes]
            _top_values(s1, v1_ref, lanes)
            _top_values(s2, v2_ref, lanes)
            v1 = v1_ref[:, lanes]
            v2 = v2_ref[:, lanes]
            cand = jnp.concatenate(
                [v1 + v2[0:1]] + [v1[0:8] + v2[j:j + 1] for j in range(1, 8)] + [v1[0:1] + v2[8:16]], axis=0)
            work = cand
            theta = None
            for _ in range(PEER_TOPK):
                theta = jnp.max(work, axis=0, keepdims=True)
                work = jnp.where(work == theta, NEG_INF, work)
            smax = v1[0:1] + v2[0:1]
            sel = cand >= theta
            z = jnp.sum(jnp.where(sel, jnp.exp(cand - smax), 0.0), axis=0, keepdims=True)
            picked = sel.astype(F32)
            cnt_ref[:, lanes] = picked[0:16]
            cnt_ref[0:8, lanes] += functools.reduce(jnp.add, [picked[8 + 8 * j:16 + 8 * j] for j in range(1, 8)])
            cnt_ref[0:1, lanes] += jnp.sum(picked[72:80], axis=0, keepdims=True)
            kk = jnp.zeros(s1.shape, F32)
            for i in range(PEER_TOPK):
                kk = jnp.where(s1 == v1_ref[i:i + 1, lanes], cnt_ref[i:i + 1, lanes], kk)
            r2 = jnp.full(s2.shape, float(PEER_TOPK), F32)
            for jr in reversed(range(PEER_TOPK)):
                r2 = jnp.where(s2 >= v2_ref[jr:jr + 1, lanes], float(jr), r2)
            r2 = r2.astype(BF16)
            e2 = jnp.exp(s2 - v2[0:1]).astype(BF16)
            for grp in range(N_KEYS // V7X_BF16_ROWS):
                r2_ref[0, h, grp, :, lanes] = r2[grp * V7X_BF16_ROWS:(grp + 1) * V7X_BF16_ROWS]
                e2_ref[0, h, grp, :, lanes] = e2[grp * V7X_BF16_ROWS:(grp + 1) * V7X_BF16_ROWS]
            kk_ref[0, h, :, lanes] = kk.astype(BF16)
            cc_ref[0, h, :, lanes] = (jnp.exp(s1 - v1[0:1]) / z).astype(BF16)


def _peer_select(hp, wqt, k1, k2, t):
    nt, d = hp.shape
    nb = nt // t
    sel_spec = pl.BlockSpec((1, PEER_HEADS, N_KEYS, t), lambda i: (i, 0, 0, 0))
    sel_shape = jax.ShapeDtypeStruct((nb, PEER_HEADS, N_KEYS, t), BF16)
    groups = N_KEYS // V7X_BF16_ROWS
    pk_spec = pl.BlockSpec((1, PEER_HEADS, groups, V7X_BF16_ROWS, t), lambda i: (i, 0, 0, 0, 0))
    pk_shape = jax.ShapeDtypeStruct((nb, PEER_HEADS, groups, V7X_BF16_ROWS, t), BF16)
    return pl.pallas_call(
        _peer_sel_kernel,
        grid=(nb,),
        in_specs=[pl.BlockSpec((t, d), lambda i: (i, 0)), pl.BlockSpec(wqt.shape, lambda i: (0, 0)),
                  pl.BlockSpec(k1.shape, lambda i: (0, 0, 0)), pl.BlockSpec(k2.shape, lambda i: (0, 0, 0))],
        out_specs=[pk_spec, pk_spec, sel_spec, sel_spec],
        out_shape=[pk_shape, pk_shape, sel_shape, sel_shape],
        scratch_shapes=[pltpu.VMEM((PEER_TOPK, t), F32), pltpu.VMEM((PEER_TOPK, t), F32),
                        pltpu.VMEM((PEER_TOPK, t), F32)],
        compiler_params=_params(("parallel",), 40 * 1024 * 1024),
        name="peer_select",
    )(hp, wqt, k1, k2)


def _gelu(x):
    return 0.5 * x * (1.0 + lax.erf(x * (2.0 ** -0.5)))


def _peer_dense_kernel(hp_ref, u_ref, vt_ref, r2_ref, e2_ref, kk_ref, cc_ref, x1_ref, g2_ref, lng_ref, lnb_ref,
                       x2_ref, acc_ref, w_ref, r2s_ref, e2s_ref, act_ref):
    j = pl.program_id(1)
    tn, t = u_ref.shape[0], hp_ref.shape[0]
    na = tn // N_KEYS

    @pl.when(j == 0)
    def _():
        acc_ref[...] = jnp.zeros_like(acc_ref)
        r2s_ref[...] = r2_ref[0]
        e2s_ref[...] = e2_ref[0]

    a_rows = pl.ds(pl.multiple_of(j * na, na), na)
    tile = (N_KEYS // V7X_BF16_ROWS, V7X_BF16_ROWS, V7X_LANES)
    for ag in range(0, na, PEER_A_GROUP):
        rows = slice(ag * N_KEYS, (ag + PEER_A_GROUP) * N_KEYS)
        act_ref[...] = _gelu(_dot_nt(u_ref[rows, :], hp_ref[...])).astype(BF16).reshape(act_ref.shape)

        for lt in range(t // V7X_LANES):
            lanes = slice(lt * V7X_LANES, (lt + 1) * V7X_LANES)
            gsum = [jnp.zeros(tile, BF16) for _ in range(PEER_A_GROUP)]
            for h in range(PEER_HEADS):
                r2 = r2s_ref[h, :, :, lanes]
                e2 = e2s_ref[h, :, :, lanes]
                kk = kk_ref[0, h, a_rows, lanes].astype(F32)
                cc = cc_ref[0, h, a_rows, lanes].astype(F32)
                for i in range(PEER_A_GROUP):
                    al = ag + i
                    kk_t = jnp.broadcast_to(kk[al:al + 1], tile[1:]).astype(BF16)[None]
                    cc_t = jnp.broadcast_to(cc[al:al + 1], tile[1:]).astype(BF16)[None]
                    gsum[i] = jnp.where(r2 < kk_t, gsum[i] + e2 * cc_t, gsum[i])
            for i in range(PEER_A_GROUP):
                w_ref[(ag + i) * tile[0]:(ag + i + 1) * tile[0], :, lanes] = (
                    gsum[i] * act_ref[i * tile[0]:(i + 1) * tile[0], :, lanes])
    acc_ref[...] += _dot(vt_ref[...], w_ref[...].reshape(tn, t))

    @pl.when(j == pl.num_programs(1) - 1)
    def _():
        y = ALPHA * x1_ref[...] + g2_ref[0] * acc_ref[...].T
        x2_ref[...] = _ln(y) * lng_ref[...] + lnb_ref[...]


def _peer_dense(hp, u, vt, layer, sel, x1, g2, blocks_per_row, lng, lnb, t, tn):
    nt, d = hp.shape
    ne = u.shape[1]
    sel_spec = pl.BlockSpec((1, PEER_HEADS, N_KEYS, t), lambda i, j: (i, 0, 0, 0))
    pk_spec = pl.BlockSpec((1, PEER_HEADS, N_KEYS // V7X_BF16_ROWS, V7X_BF16_ROWS, t), lambda i, j: (i, 0, 0, 0, 0))
    fix = lambda i, j: (0, 0)
    return pl.pallas_call(
        _peer_dense_kernel,
        grid=(nt // t, ne // tn),
        in_specs=[pl.BlockSpec((t, d), lambda i, j: (i, 0)), pl.BlockSpec((None, tn, d), lambda i, j: (layer, j, 0)),
                  pl.BlockSpec((None, d, tn), lambda i, j: (layer, 0, j)), pk_spec, pk_spec, sel_spec, sel_spec,
                  pl.BlockSpec((t, d), lambda i, j: (i, 0)),
                  pl.BlockSpec((1, 1, d), lambda i, j: (i // blocks_per_row, 0, 0)),
                  pl.BlockSpec((1, d), fix), pl.BlockSpec((1, d), fix)],
        out_specs=pl.BlockSpec((t, d), lambda i, j: (i, 0)),
        out_shape=jax.ShapeDtypeStruct((nt, d), F32),
        scratch_shapes=[pltpu.VMEM((d, t), F32), pltpu.VMEM((tn // V7X_BF16_ROWS, V7X_BF16_ROWS, t), BF16),
                        pltpu.VMEM(pk_spec.block_shape[1:], BF16), pltpu.VMEM(pk_spec.block_shape[1:], BF16),
                        pltpu.VMEM((PEER_A_GROUP * N_KEYS // V7X_BF16_ROWS, V7X_BF16_ROWS, t), BF16)],
        compiler_params=_params(("parallel", "arbitrary"), 48 * 1024 * 1024),
        name="peer_dense",
    )(hp, u, vt, *sel, x1, g2, lng, lnb)


def _peer_block(x1, hp, g2, wqt, k1, k2, u, vt, layer, lng, lnb):
    bn, l, d = x1.shape
    t = min(512, l)
    tn = 16 * N_KEYS
    hp2 = hp.reshape(bn * l, d)
    sel = _peer_select(hp2, wqt, k1, k2, t)
    x2 = _peer_dense(hp2, u, vt, layer, sel, x1.reshape(bn * l, d), g2, l // t, lng, lnb, t, tn)
    return x2.reshape(bn, l, d)


def _rope_tables(l, att_w):
    rows = l // GRID_W
    row = jnp.repeat(jnp.arange(rows, dtype=F32), GRID_W)
    col = jnp.tile(jnp.arange(GRID_W, dtype=F32), rows)
    f = HEAD_DIM // 4
    inv_freq = ROPE_BASE ** (-jnp.arange(f, dtype=F32) / f)
    ar = row[:, None] * inv_freq[None, :]
    ac = col[:, None] * inv_freq[None, :]
    cos = jnp.concatenate([jnp.cos(ar), jnp.cos(ar), jnp.cos(ac), jnp.cos(ac)], axis=-1)
    sin = jnp.concatenate([-jnp.sin(ar), jnp.sin(ar), -jnp.sin(ac), jnp.sin(ac)], axis=-1)
    reps = att_w // HEAD_DIM
    return jnp.tile(cos, (1, reps)), jnp.tile(sin, (1, reps))


def _block_diag_ones(width, group):
    idx = jnp.arange(width) // group
    return (idx[:, None] == idx[None, :]).astype(BF16)


def kernel(x, c, ctx, c_ctx, w_mod, b_mod, ln1_g, ln1_b, ln2_g, ln2_b, ab_w_in, ab_b_in, ab_conv_w, ab_conv_b,
           ab_conv_ln_g, ab_conv_ln_b, ab_q_norm_g, ab_k_norm_g, ab_w_out, ab_b_out, ml_w_in, ml_b_in, ml_norm_g,
           ml_w_out, ml_b_out, peer_w_q, peer_k1, peer_k2, peer_u, peer_v):
    bsz, l, d = x.shape
    lc = ctx.shape[1]
    assert l % GRID_W == 0 and lc % ML_CHUNK == 0 and l % lc == 0

    rows = -(-(bsz + 1) // 8) * 8
    c_rows = jnp.zeros((rows, d), F32).at[:bsz].set(c).at[bsz].set(c_ctx)
    mod = _modulation(c_rows, w_mod, b_mod)

    def mod_rows(i):
        m = mod[i].reshape(rows, N_MOD, d)
        lat = [m[:bsz, k][:, None, :] for k in range(N_MOD)]
        cx = [m[bsz:bsz + 1, k][:, None, :] for k in range(N_MOD)]
        return lat, cx

    row2 = lambda v: v.reshape(1, -1)

    conv_ch = ab_conv_w.shape[2]
    kv_w = ATT_KV_HEADS * HEAD_DIM
    att_w = ATT_HEADS * HEAD_DIM
    (sh1l, sc1l, g1l, sh2l, sc2l, g2l), (sh1c, sc1c, g1c, sh2c, sc2c, g2c) = mod_rows(0)
    bcast = lambda v: jnp.broadcast_to(v, (bsz, 1, d))
    w_in = ab_w_in[0].astype(BF16)
    b_in = row2(ab_b_in[0])
    qg = row2(jnp.tile(ab_q_norm_g[0], ATT_HEADS))
    kg = row2(jnp.tile(ab_k_norm_g[0], ATT_KV_HEADS))
    bd = _block_diag_ones(2 * V7X_LANES, HEAD_DIM)
    cos, sin = _rope_tables(l, att_w)
    ul, ql, kl, vl = _ab_in(x, sh1l, sc1l, w_in, b_in, qg, kg, bd, cos, sin, True, conv_ch, att_w, kv_w)
    uc, qc, kc, vc = _ab_in(ctx, bcast(sh1c), bcast(sc1c), w_in, b_in, qg, kg, bd, cos[:lc], sin[:lc], False,
                            conv_ch, att_w, kv_w)
    conv_args = (ab_conv_w[0], row2(ab_conv_b[0]), row2(ab_conv_ln_g[0]), row2(ab_conv_ln_b[0]))
    conv_l = _conv_group(ul, *conv_args)
    conv_c = _conv_group(uc, *conv_args)
    att_l = _attention(ql, jnp.concatenate([kc, kl], axis=1), jnp.concatenate([vc, vl], axis=1))
    att_c = _attention(qc, kc, vc)
    w_out = ab_w_out[0].astype(BF16)
    out_args = (w_out[:conv_ch], w_out[conv_ch:], row2(ab_b_out[0]))
    ln1 = (row2(ln1_g[0]), row2(ln1_b[0]))
    x1, hpl = _ab_out(conv_l, att_l, *out_args, x, g1l, *ln1, sh2l, sc2l)
    c1, hpc = _ab_out(conv_c, att_c, *out_args, ctx, bcast(g1c), *ln1, bcast(sh2c), bcast(sc2c))

    assert peer_k1.shape[3] == N_KEYS
    u_all = peer_u.astype(BF16)
    vt_all = jnp.swapaxes(peer_v, 1, 2).astype(BF16)

    def peer_weights(i):
        wqt = peer_w_q[i].T.astype(BF16)
        return (wqt, peer_k1[i].astype(BF16), peer_k2[i].astype(BF16), u_all, vt_all, i,
                row2(ln2_g[i]), row2(ln2_b[i]))

    pw = peer_weights(0)
    x = _peer_block(x1, hpl, g2l, *pw)
    ctx = _peer_block(c1.reshape(1, bsz * lc, d), hpc.reshape(1, bsz * lc, d), g2c, *pw).reshape(bsz, lc, d)

    (sh1l, sc1l, g1l, sh2l, sc2l, g2l), (sh1c, sc1c, _, _, _, _) = mod_rows(1)
    qk_w = ML_HEADS * (d // 8)
    v_w = ML_HEADS * (d // 4)
    n_main = 2 * qk_w + 2 * v_w
    w_in = ml_w_in[0]
    wg = jnp.zeros((d, V7X_LANES), F32).at[:, :4 * ML_HEADS].set(w_in[:, n_main:]).astype(BF16)
    bg = jnp.zeros((1, V7X_LANES), F32).at[:, :4 * ML_HEADS].set(ml_b_in[0][n_main:])
    xcat = jnp.concatenate([ctx, x], axis=1)
    q, k, v, o, g = _ml_in(xcat, lc, sh1l, sc1l, sh1c, sc1c, w_in[:, :n_main].astype(BF16),
                           row2(ml_b_in[0][:n_main]), wg, bg, qk_w, v_w)
    s = lc + l
    g2 = g.reshape(bsz, s, 2, 2 * ML_HEADS).transpose(2, 0, 1, 3)
    gt2 = g2.reshape(2, bsz, s // ML_CHUNK, ML_CHUNK, 2 * ML_HEADS).transpose(0, 1, 2, 4, 3)
    hf, hb = _mlstm(q, k, v, g2, gt2, lc // ML_CHUNK)
    x1, hpl = _ml_out(hf, hb, o, lc, row2(ml_norm_g[0]), ml_w_out[0].astype(BF16), row2(ml_b_out[0]), x, g1l,
                      row2(ln1_g[1]), row2(ln1_b[1]), sh2l, sc2l)
    return _peer_block(x1, hpl, g2l, *peer_weights(1))
```
